```python
import math
import jax
import jax.numpy as jnp
from jax import lax
import numpy as np

D_MODEL = 1024
BATCH = 4
SEQ = 8192
DEPTH = 4

GRID_W = 64
CTX_LEN = 256
N_MIXERS = 3
EPS = 1e-6
N_MOD = 9
D_FF = 2816
MACARON_W = 0.5
D_RNN = 1280
LRU_BLOCKS = 8
LRU_BLK = D_RNN // LRU_BLOCKS
CONV_W = 4
CONV_PAD_LEFT = 1
LRU_C = 8.0
LRU_A_MIN = 0.9
LRU_A_MAX = 0.999
RET_HEADS = 4
RET_DK = D_MODEL // RET_HEADS
RET_DV = 2 * RET_DK
RET_CHUNK = 128
RET_THETA_BASE = 10000.0
DIFF_HEADS = 8
DIFF_DH = D_MODEL // DIFF_HEADS // 2
DIFF_DV = 2 * DIFF_DH
Q_BLOCK = 128
ROPE_BASE = 10000.0

kernel_name = 'hybrid_rglru_retention_diffattn_macaron_dit'


def _rms(x):
    xf = x.astype(jnp.float32)
    return xf * lax.rsqrt(jnp.mean(xf * xf, axis=-1, keepdims=True) + EPS)


def rms_norm(x, g):
    return (_rms(x) * g.astype(jnp.float32)).astype(x.dtype)


def modulate(h, shift, scale):
    return h * (1 + scale) + shift


def ada_mod(cond, w, b):
    m = jax.nn.silu(cond) @ w + b
    return jnp.split(m[..., None, :], N_MOD, axis=-1)


def swiglu(h, w_in, w_out):
    a, g = jnp.split(h @ w_in, 2, axis=-1)
    return (jax.nn.silu(a) * g) @ w_out


def apply_rope(x, cos, sin):
    x1, x2 = jnp.split(x.astype(jnp.float32), 2, axis=-1)
    c = cos[:, None, :]
    s = sin[:, None, :]
    return jnp.concatenate([x1 * c - x2 * s, x1 * s + x2 * c], axis=-1).astype(x.dtype)


def axial_rope_tables(rows, head_dim):
    n_freq = head_dim // 4
    inv = ROPE_BASE ** (-jnp.arange(n_freq, dtype=jnp.float32) / n_freq)
    row = jnp.broadcast_to(jnp.arange(rows, dtype=jnp.float32)[:, None], (rows, GRID_W)).reshape(-1)
    col = jnp.broadcast_to(jnp.arange(GRID_W, dtype=jnp.float32)[None, :], (rows, GRID_W)).reshape(-1)
    ang = jnp.concatenate([row[:, None] * inv, col[:, None] * inv], axis=-1)
    return jnp.cos(ang), jnp.sin(ang)


def retention_rope_tables(n_tokens):
    theta = RET_THETA_BASE ** (-jnp.linspace(0.0, 1.0, RET_DK // 2, dtype=jnp.float32))
    ang = jnp.arange(n_tokens, dtype=jnp.float32)[:, None] * theta
    return jnp.cos(ang), jnp.sin(ang)


def block_diag(x, w, b):
    bsz, t, _ = x.shape
    xg = x.reshape(bsz, t, LRU_BLOCKS, LRU_BLK)
    return (jnp.einsum('btgi,gij->btgj', xg, w) + b).reshape(bsz, t, D_RNN)


def lru_conv_branch(h, w_x, conv_w, conv_b):
    xb = h @ w_x
    xc = lax.conv_general_dilated(
        xb, conv_w[:, None, :].astype(xb.dtype), window_strides=(1,),
        padding=[(CONV_PAD_LEFT, CONV_W - 1 - CONV_PAD_LEFT)],
        dimension_numbers=('NWC', 'WIO', 'NWC'), feature_group_count=D_RNN)
    return xc + conv_b


def lru_coeffs(xc, gate_w, gate_b, lam):
    xf = xc.astype(jnp.float32)
    r = jax.nn.sigmoid(block_diag(xf, gate_w[0], gate_b[0]))
    ig = jax.nn.sigmoid(block_diag(xf, gate_w[1], gate_b[1]))
    log_a = -LRU_C * r * jax.nn.softplus(-lam.astype(jnp.float32))
    a = jnp.exp(log_a)
    b = jnp.sqrt(-jnp.expm1(2.0 * log_a)) * (ig * xf)
    return a, b


def _affine_combine(left, right):
    a_l, b_l = left
    a_r, b_r = right
    return a_l * a_r, a_r * b_l + b_r


def linear_scan(a, b, h0, reverse):
    if reverse:
        a, b = jnp.flip(a, 1), jnp.flip(b, 1)
    b = b.at[:, 0].add(a[:, 0] * h0)
    _, h = lax.associative_scan(_affine_combine, (a, b), axis=1)
    return jnp.flip(h, 1) if reverse else h


def mixer_rglru(hc, hx, w_in, conv_w, conv_b, gate_w, gate_b, lam, w_out, need_ctx_out):
    w_x, w_g = w_in[:, :D_RNN], w_in[:, D_RNN:]
    xc_c = lru_conv_branch(hc, w_x, conv_w, conv_b)
    xc_x = lru_conv_branch(hx, w_x, conv_w, conv_b)
    h0 = jnp.zeros((hc.shape[0], D_RNN), jnp.float32)
    hs_c, hs_x = [], []
    for d, rev in enumerate((False, True)):
        a_c, b_c = lru_coeffs(xc_c, gate_w[d], gate_b[d], lam[d])
        h_c = linear_scan(a_c, b_c, h0, rev)
        h_end = h_c[:, 0] if rev else h_c[:, -1]
        a_x, b_x = lru_coeffs(xc_x, gate_w[d], gate_b[d], lam[d])
        hs_x.append(linear_scan(a_x, b_x, h_end, rev))
        hs_c.append(h_c)

    def finish(h, hsum):
        gate = jax.nn.gelu((h @ w_g).astype(jnp.float32))
        return (hsum * gate).astype(h.dtype) @ w_out

    yx = finish(hx, hs_x[0] + hs_x[1])
    yc = finish(hc, hs_c[0] + hs_c[1]) if need_ctx_out else None
    return yc, yx


def retention_scan(q, k, v, log_gamma, s0, include_diag):
    bsz, t, heads, _ = q.shape
    dv = v.shape[-1]
    n = t // RET_CHUNK
    idx = jnp.arange(RET_CHUNK, dtype=jnp.float32)
    dist = idx[:, None] - idx[None, :]
    mask = (dist >= 0) if include_diag else (dist > 0)
    intra = jnp.where(mask[None], jnp.exp(log_gamma[:, None, None] * jnp.maximum(dist, 0.0)[None]), 0.0)
    q_decay = jnp.exp(log_gamma[:, None] * (idx[None] + 1.0)).T[None, :, :, None]
    k_decay = jnp.exp(log_gamma[:, None] * (RET_CHUNK - 1.0 - idx[None])).T[None, :, :, None]
    chunk_decay = jnp.exp(log_gamma * RET_CHUNK)[None, :, None, None]

    def to_chunks(z):
        return jnp.moveaxis(z.reshape(bsz, n, RET_CHUNK, heads, z.shape[-1]), 1, 0)

    def step(s, qkv):
        qc, kc, vc = (z.astype(jnp.float32) for z in qkv)
        scores = jnp.einsum('bihd,bjhd->bhij', qc, kc) * intra
        o = jnp.einsum('bhij,bjhe->bihe', scores, vc)
        o = o + jnp.einsum('bihd,bhde->bihe', qc * q_decay, s)
        s = s * chunk_decay + jnp.einsum('bjhd,bjhe->bhde', kc * k_decay, vc)
        return s, o

    s_fin, o = lax.scan(step, s0, (to_chunks(q), to_chunks(k), to_chunks(v)))
    return jnp.moveaxis(o, 0, 1).reshape(bsz, t, heads, dv), s_fin


def bi_retention(q, k, v, log_gamma, s0_f, s0_b):
    o_f, s_f = retention_scan(q, k, v, log_gamma, s0_f, True)
    flip = lambda z: jnp.flip(z, axis=1)
    o_b, s_b = retention_scan(flip(q), flip(k), flip(v), log_gamma, s0_b, False)
    return o_f + flip(o_b), s_f, s_b


def mixer_retention(hc, hx, w_in, w_out, cos_r, sin_r, need_ctx_out):
    log_gamma = jnp.log1p(-jnp.exp2(-5.0 - jnp.arange(RET_HEADS, dtype=jnp.float32)))
    hdk = RET_HEADS * RET_DK
    hdv = RET_HEADS * RET_DV

    def project(h):
        bsz, t, _ = h.shape
        q, k, v, g = jnp.split(h @ w_in, [hdk, 2 * hdk, 2 * hdk + hdv], axis=-1)
        q = q.reshape(bsz, t, RET_HEADS, RET_DK)
        k = k.reshape(bsz, t, RET_HEADS, RET_DK) * (RET_DK ** -0.5)
        v = v.reshape(bsz, t, RET_HEADS, RET_DV)
        return q, k, v, g

    def finish(o, g):
        bsz, t = g.shape[:2]
        y = _rms(o).reshape(bsz, t, hdv) * jax.nn.silu(g.astype(jnp.float32))
        return y.astype(g.dtype) @ w_out

    qc, kc, vc, gc = project(hc)
    qx, kx, vx, gx = project(hx)
    qx = apply_rope(qx, cos_r, sin_r)
    kx = apply_rope(kx, cos_r, sin_r)
    zero = jnp.zeros((hc.shape[0], RET_HEADS, RET_DK, RET_DV), jnp.float32)
    oc, s_f, s_b = bi_retention(qc, kc, vc, log_gamma, zero, zero)
    ox, _, _ = bi_retention(qx, kx, vx, log_gamma, s_f, s_b)
    yx = finish(ox, gx)
    yc = finish(oc, gc) if need_ctx_out else None
    return yc, yx


def diff_attend(q, k, v, lam):
    s = jnp.einsum('bqhmd,bkhmd->bhmqk', q.astype(jnp.float32), k.astype(jnp.float32)) * (DIFF_DH ** -0.5)
    p = jax.nn.softmax(s, axis=-1)
    w = p[:, :, 0] - lam * p[:, :, 1]
    return jnp.einsum('bhqk,bkhe->bqhe', w, v.astype(jnp.float32))


def mixer_diff(hc, hx, w_in, lam_p, subln_g, w_out, lam_init, cos_d, sin_d, need_ctx_out):
    lp = lam_p.astype(jnp.float32)
    lam = jnp.exp(jnp.sum(lp[0] * lp[1])) - jnp.exp(jnp.sum(lp[2] * lp[3])) + lam_init

    def project(h, rotate):
        bsz, t, _ = h.shape
        q, k, v = jnp.split(h @ w_in, 3, axis=-1)
        q = q.reshape(bsz, t, 2 * DIFF_HEADS, DIFF_DH)
        k = k.reshape(bsz, t, 2 * DIFF_HEADS, DIFF_DH)
        if rotate:
            q = apply_rope(q, cos_d, sin_d)
            k = apply_rope(k, cos_d, sin_d)
        return (q.reshape(bsz, t, DIFF_HEADS, 2, DIFF_DH), k.reshape(bsz, t, DIFF_HEADS, 2, DIFF_DH),
                v.reshape(bsz, t, DIFF_HEADS, DIFF_DV))

    def finish(o, dtype):
        bsz, t = o.shape[:2]
        y = rms_norm(o, subln_g) * (1.0 - lam_init)
        return y.reshape(bsz, t, DIFF_HEADS * DIFF_DV).astype(dtype) @ w_out

    qc, kc, vc = project(hc, False)
    qx, kx, vx = project(hx, True)
    k_all = jnp.concatenate([kc, kx], axis=1)
    v_all = jnp.concatenate([vc, vx], axis=1)
    bsz, t = qx.shape[:2]
    n_blk = t // Q_BLOCK
    q_blocks = jnp.moveaxis(qx.reshape(bsz, n_blk, Q_BLOCK, DIFF_HEADS, 2, DIFF_DH), 1, 0)
    ox = lax.map(lambda qb: diff_attend(qb, k_all, v_all, lam), q_blocks)
    ox = jnp.moveaxis(ox, 0, 1).reshape(bsz, t, DIFF_HEADS, DIFF_DV)
    yx = finish(ox, hx.dtype)
    yc = finish(diff_attend(qc, kc, vc, lam), hc.dtype) if need_ctx_out else None
    return yc, yx


def setup_inputs(seed: int = 0) -> dict:
    key = jax.random.key(seed)
    keys = iter(jax.random.split(key, 32))
    f32 = jnp.float32

    def normal(shape, scale):
        return jax.random.normal(next(keys), shape, f32) * scale

    n_a = len(range(0, DEPTH, N_MIXERS))
    n_b = len(range(1, DEPTH, N_MIXERS))
    n_c = len(range(2, DEPTH, N_MIXERS))
    u = jax.random.uniform(next(keys), (n_a, 2, D_RNN), f32, LRU_A_MIN ** 2, LRU_A_MAX ** 2)
    a_base = u ** (1.0 / LRU_C)
    lru_lam = jnp.log(a_base) - jnp.log1p(-a_base)
    return {
        'x': normal((BATCH, SEQ, D_MODEL), 1.0),
        'c': normal((BATCH, D_MODEL), 1.0),
        'ctx': normal((BATCH, CTX_LEN, D_MODEL), 1.0),
        'c_ctx': normal((D_MODEL,), 1.0),
        'norm_g': 1.0 + normal((DEPTH, 3, D_MODEL), 0.05),
        'mod_w': normal((DEPTH, D_MODEL, N_MOD * D_MODEL), 0.5 * D_MODEL ** -0.5),
        'mod_b': normal((DEPTH, N_MOD * D_MODEL), 0.02),
        'ffn_w_in': normal((DEPTH, 2, D_MODEL, 2 * D_FF), D_MODEL ** -0.5),
        'ffn_w_out': normal((DEPTH, 2, D_FF, D_MODEL), D_FF ** -0.5),
        'lru_w_in': normal((n_a, D_MODEL, 2 * D_RNN), D_MODEL ** -0.5),
        'lru_conv_w': normal((n_a, CONV_W, D_RNN), CONV_W ** -0.5),
        'lru_conv_b': normal((n_a, D_RNN), 0.02),
        'lru_gate_w': normal((n_a, 2, 2, LRU_BLOCKS, LRU_BLK, LRU_BLK), LRU_BLK ** -0.5),
        'lru_gate_b': normal((n_a, 2, 2, LRU_BLOCKS, LRU_BLK), 0.02),
        'lru_lam': lru_lam,
        'lru_w_out': normal((n_a, D_RNN, D_MODEL), D_RNN ** -0.5),
        'ret_w_in': normal((n_b, D_MODEL, 2 * RET_HEADS * (RET_DK + RET_DV)), D_MODEL ** -0.5),
        'ret_w_out': normal((n_b, RET_HEADS * RET_DV, D_MODEL), (RET_HEADS * RET_DV) ** -0.5),
        'dif_w_in': normal((n_c, D_MODEL, 3 * D_MODEL), D_MODEL ** -0.5),
        'dif_lam': normal((n_c, 4, DIFF_DH), 0.1),
        'dif_subln': 1.0 + normal((n_c, DIFF_DV), 0.05),
        'dif_w_out': normal((n_c, DIFF_HEADS * DIFF_DV, D_MODEL), (DIFF_HEADS * DIFF_DV) ** -0.5),
        'final_g': 1.0 + normal((D_MODEL,), 0.05),
    }


def reference(x, c, ctx, c_ctx, norm_g, mod_w, mod_b, ffn_w_in, ffn_w_out,
              lru_w_in, lru_conv_w, lru_conv_b, lru_gate_w, lru_gate_b, lru_lam, lru_w_out,
              ret_w_in, ret_w_out, dif_w_in, dif_lam, dif_subln, dif_w_out, final_g):
    n_lat = x.shape[1]
    rows = n_lat // GRID_W
    cos_d, sin_d = axial_rope_tables(rows, DIFF_DH)
    cos_r, sin_r = retention_rope_tables(n_lat)
    h_ctx = ctx
    for i in range(DEPTH):
        last = i == DEPTH - 1
        mx = ada_mod(c, mod_w[i], mod_b[i])
        mc = ada_mod(c_ctx, mod_w[i], mod_b[i])
        x = x + MACARON_W * mx[2] * swiglu(modulate(rms_norm(x, norm_g[i, 0]), mx[0], mx[1]),
                                           ffn_w_in[i, 0], ffn_w_out[i, 0])
        h_ctx = h_ctx + MACARON_W * mc[2] * swiglu(modulate(rms_norm(h_ctx, norm_g[i, 0]), mc[0], mc[1]),
                                                   ffn_w_in[i, 0], ffn_w_out[i, 0])
        hx = modulate(rms_norm(x, norm_g[i, 1]), mx[3], mx[4])
        hc = modulate(rms_norm(h_ctx, norm_g[i, 1]), mc[3], mc[4])
        kind, j = i % N_MIXERS, i // N_MIXERS
        if kind == 0:
            yc, yx = mixer_rglru(hc, hx, lru_w_in[j], lru_conv_w[j], lru_conv_b[j], lru_gate_w[j],
                                 lru_gate_b[j], lru_lam[j], lru_w_out[j], not last)
        elif kind == 1:
            yc, yx = mixer_retention(hc, hx, ret_w_in[j], ret_w_out[j], cos_r, sin_r, not last)
        else:
            lam_init = 0.8 - 0.6 * math.exp(-0.3 * i)
            yc, yx = mixer_diff(hc, hx, dif_w_in[j], dif_lam[j], dif_subln[j], dif_w_out[j],
                                lam_init, cos_d, sin_d, not last)
        x = x + mx[5] * yx
        x = x + MACARON_W * mx[8] * swiglu(modulate(rms_norm(x, norm_g[i, 2]), mx[6], mx[7]),
                                           ffn_w_in[i, 1], ffn_w_out[i, 1])
        if not last:
            h_ctx = h_ctx + mc[5] * yc
            h_ctx = h_ctx + MACARON_W * mc[8] * swiglu(modulate(rms_norm(h_ctx, norm_g[i, 2]), mc[6], mc[7]),
                                                       ffn_w_in[i, 1], ffn_w_out[i, 1])
    return rms_norm(x, final_g)
```

```python
import functools
import math

import numpy as np
import jax
import jax.numpy as jnp
from jax import lax
from jax.experimental import pallas as pl
from jax.experimental.pallas import tpu as pltpu

F32 = jnp.float32
MXU_DTYPE = jnp.bfloat16

EPS = 1e-6
N_MOD = 9
MACARON_W = 0.5
GRID_W = 64
ROPE_BASE = 10000.0
LRU_BLOCKS = 8
LRU_GROUPS = 2
CONV_W = 4
CONV_PAD_LEFT = 1
LRU_C = 8.0
LRU_CHUNK = 128
LRU_ROW_TILE = 256
RET_HEADS = 4
RET_CHUNK = 128
RET_THETA_BASE = 10000.0
DIFF_HEADS = 8
ATT_TQ = 512
ATT_TK = 256

LANES = 128
SUBLANES = 8
VMEM_LIMIT = 56 * 1024 * 1024


def _params(sem):
    return pltpu.CompilerParams(dimension_semantics=sem, vmem_limit_bytes=VMEM_LIMIT)


def _resident(shape, index_map):
    return pl.BlockSpec(shape, index_map, pipeline_mode=pl.Buffered(1))


def _mx(v):
    return v.astype(MXU_DTYPE)


def _dot(a, b):
    return jnp.dot(a, b, preferred_element_type=F32)


def _dot_nt(a, b):
    return lax.dot_general(a, b, (((1,), (1,)), ((), ())), preferred_element_type=F32)


def _norm_mod(x, g, shift, scale):
    ms = jnp.mean(x * x, axis=-1, keepdims=True)
    return (x * lax.rsqrt(ms + EPS)) * g * (1.0 + scale) + shift


def _silu(v):
    return v * jax.nn.sigmoid(v)


class _Rows:
    def __init__(self, bsz, n_ctx, n_lat):
        self.b, self.c, self.t = bsz, n_ctx, n_lat
        self.total = bsz * (n_ctx + n_lat)

    def tile(self, cap):
        return math.gcd(math.gcd(self.b * self.c, self.t), cap)

    def mod_row(self, tm):
        n_ctx_tiles = self.b * self.c // tm
        per_batch = self.t // tm
        bsz = self.b
        return lambda i: jnp.where(i < n_ctx_tiles, bsz, (i - n_ctx_tiles) // per_batch)

    def pos_block(self, tm):
        n_ctx_tiles = self.b * self.c // tm
        per_batch = self.t // tm
        return lambda i: jnp.where(i < n_ctx_tiles, 0, 1 + (i - n_ctx_tiles) % per_batch)

    def seq_of_tile(self, tm):
        n_ctx_tiles = self.b * self.c // tm
        ctx_per, lat_per = self.c // tm, self.t // tm

        def f(i):
            r = i - n_ctx_tiles
            is_ctx = i < n_ctx_tiles
            return (jnp.where(is_ctx, i // ctx_per, r // lat_per),
                    jnp.where(is_ctx, i % ctx_per, ctx_per + r % lat_per))
        return f

    def seq_block(self, chunk, reverse):
        ncc, ncl = self.c // chunk, self.t // chunk
        bsz = self.b

        def f(b, s):
            if reverse:
                q_ctx, q_lat = ncc - 1 - s, ncl - 1 - (s - ncc)
            else:
                q_ctx, q_lat = s, s - ncc
            return jnp.where(s < ncc, b * ncc + q_ctx, bsz * ncc + b * ncl + q_lat)
        return f


def _ada_kernel(c_ref, w_ref, b_ref, o_ref):
    s = _mx(_silu(c_ref[...]))
    o_ref[0] = _dot(s, _mx(w_ref[0])) + b_ref[0]


def _ada_mod(cond, mod_w, mod_b):
    depth, d, n = mod_w.shape
    tn = 1024
    return pl.pallas_call(
        _ada_kernel,
        grid=(depth, n // tn),
        in_specs=[
            pl.BlockSpec((SUBLANES, d), lambda l, j: (0, 0)),
            pl.BlockSpec((1, d, tn), lambda l, j: (l, 0, j)),
            pl.BlockSpec((1, 1, tn), lambda l, j: (l, 0, j)),
        ],
        out_specs=pl.BlockSpec((1, SUBLANES, tn), lambda l, j: (l, 0, j)),
        out_shape=jax.ShapeDtypeStruct((depth, SUBLANES, n), F32),
        compiler_params=_params(("parallel", "parallel")),
        name="ada_mod",
    )(cond, mod_w, mod_b.reshape(depth, 1, n))


def _ffn_kernel(x_ref, mod_ref, g_ref, win_ref, wout_ref, o_ref, *, k0, n_chunks):
    x = x_ref[...]
    h = _mx(_norm_mod(x, g_ref[...], mod_ref[0, k0:k0 + 1, :], mod_ref[0, k0 + 1:k0 + 2, :]))
    acc = jnp.zeros(x.shape, F32)
    for c in range(n_chunks):
        a = _dot(h, win_ref[0, c])
        g = _dot(h, win_ref[1, c])
        acc = acc + _dot(_mx(_silu(a) * g), wout_ref[c])
    o_ref[...] = x + (MACARON_W * mod_ref[0, k0 + 2:k0 + 3, :]) * acc


def _ffn(rows, x, mod, g, w_in, w_out, k0):
    d = x.shape[1]
    n_chunks, fc = w_out.shape[0], w_out.shape[1]
    tm = rows.tile(512)
    return pl.pallas_call(
        functools.partial(_ffn_kernel, k0=k0, n_chunks=n_chunks),
        grid=(rows.total // tm,),
        in_specs=[
            pl.BlockSpec((tm, d), lambda i: (i, 0)),
            pl.BlockSpec((1, N_MOD, d), lambda i, f=rows.mod_row(tm): (f(i), 0, 0)),
            pl.BlockSpec((1, d), lambda i: (0, 0)),
            _resident((2, n_chunks, d, fc), lambda i: (0, 0, 0, 0)),
            _resident((n_chunks, fc, d), lambda i: (0, 0, 0)),
        ],
        out_specs=pl.BlockSpec((tm, d), lambda i: (i, 0)),
        out_shape=jax.ShapeDtypeStruct(x.shape, F32),
        compiler_params=_params(("parallel",)),
        name="ffn",
    )(x, mod, g, w_in, w_out)


def _ffn_weights(w_in, w_out, fc):
    d, two_f = w_in.shape
    f = two_f // 2
    w_in = _mx(w_in).reshape(d, 2, f // fc, fc).transpose(1, 2, 0, 3)
    return w_in, _mx(w_out).reshape(f // fc, fc, d)


def _lru_in_kernel(x_ref, mod_ref, g_ref, w_ref, xb_ref, gate_ref, *, d_rnn):
    h = _mx(_norm_mod(x_ref[...], g_ref[...], mod_ref[0, 3:4, :], mod_ref[0, 4:5, :]))
    xb_ref[0] = _dot(h, w_ref[:, :d_rnn])
    gate_ref[...] = jax.nn.gelu(_dot(h, w_ref[:, d_rnn:])).astype(gate_ref.dtype)


def _lru_in(rows, x, mod, g, w_in):
    d = x.shape[1]
    d_rnn = w_in.shape[1] // 2
    tm = rows.tile(LRU_ROW_TILE)
    seq = rows.seq_of_tile(tm)
    return pl.pallas_call(
        functools.partial(_lru_in_kernel, d_rnn=d_rnn),
        grid=(rows.total // tm,),
        in_specs=[
            pl.BlockSpec((tm, d), lambda i: (i, 0)),
            pl.BlockSpec((1, N_MOD, d), lambda i, f=rows.mod_row(tm): (f(i), 0, 0)),
            pl.BlockSpec((1, d), lambda i: (0, 0)),
            _resident((d, 2 * d_rnn), lambda i: (0, 0)),
        ],
        out_specs=[pl.BlockSpec((1, tm, d_rnn), lambda i: seq(i) + (0,)),
                   pl.BlockSpec((tm, d_rnn), lambda i: (i, 0))],
        out_shape=[jax.ShapeDtypeStruct((rows.b, rows.c + rows.t, d_rnn), F32),
                   jax.ShapeDtypeStruct((rows.total, d_rnn), MXU_DTYPE)],
        compiler_params=_params(("parallel",)),
        name="lru_in",
    )(x, mod, g, w_in)


def _lru_scan_kernel(main_ref, prev_ref, next_ref, cw_ref, cb_ref, gw_ref, gb_ref, lam_ref, out_ref,
                     ext_ref, a_ref, b_ref, h_ref, *, ncc, ncl, pitch):
    bsz, tc, d_rnn = main_ref.shape
    d = pl.program_id(0)
    s = pl.program_id(1)
    n_slabs = d_rnn // LANES
    gw = d_rnn // LRU_GROUPS

    @pl.when(s == 0)
    def _():
        h_ref[...] = jnp.zeros(h_ref.shape, F32)

    q_rev = jnp.where(s < ncc, ncc - 1 - s, ncl - 1 - (s - ncc))
    q_fwd = jnp.where(s < ncc, s, s - ncc)
    q = jnp.where(d == 0, q_fwd, q_rev)
    n_seg = jnp.where(s < ncc, ncc, ncl)
    has_prev = (q > 0).astype(F32)
    has_next = (q < n_seg - 1).astype(F32)

    nl = -lam_ref[0]
    sp = jnp.maximum(nl, 0.0) + jnp.log1p(jnp.exp(-jnp.abs(nl)))

    for b in range(bsz):
        ext_ref[0:SUBLANES, :] = prev_ref[b] * has_prev
        ext_ref[SUBLANES:SUBLANES + tc, :] = main_ref[b]
        ext_ref[SUBLANES + tc:, :] = next_ref[b] * has_next
        base = SUBLANES - CONV_PAD_LEFT
        xc = cb_ref[...] + cw_ref[0:1, :] * ext_ref[base:base + tc, :]
        for k in range(1, CONV_W):
            xc = xc + cw_ref[k:k + 1, :] * ext_ref[base + k:base + k + tc, :]
        for gi in range(LRU_GROUPS):
            xg = xc[:, gi * gw:(gi + 1) * gw]
            z = _dot(_mx(xg), gw_ref[0, gi]) + gb_ref[0, gi]
            r = jax.nn.sigmoid(z[:, :gw])
            ig = jax.nn.sigmoid(z[:, gw:])
            log_a = (-LRU_C) * r * sp[:, gi * gw:(gi + 1) * gw]
            a = jnp.exp(log_a)
            bb = jnp.sqrt(1.0 - a * a) * (ig * xg)
            for j in range(gw // LANES):
                slab = gi * (gw // LANES) + j
                a_ref[slab, b * pitch:b * pitch + tc, :] = a[:, j * LANES:(j + 1) * LANES]
                b_ref[slab, b * pitch:b * pitch + tc, :] = bb[:, j * LANES:(j + 1) * LANES]

    def step(t, hs):
        row = t + d * (tc - 1 - 2 * t)
        new = []
        for slab in range(n_slabs):
            idx = (slab, pl.ds(row, bsz, stride=pitch), slice(None))
            h = a_ref[idx] * hs[slab] + b_ref[idx]
            b_ref[idx] = h
            new.append(h)
        return tuple(new)

    hs = lax.fori_loop(0, tc, step, tuple(h_ref[slab] for slab in range(n_slabs)), unroll=8)
    for slab in range(n_slabs):
        h_ref[slab] = hs[slab]

    for b in range(bsz):
        for slab in range(n_slabs):
            out_ref[0, b, :, slab * LANES:(slab + 1) * LANES] = b_ref[slab, b * pitch:b * pitch + tc, :]


def _lru_scan(rows, xb, conv_w, conv_b, gate_w, gate_b, lam):
    bsz, n_seq, d_rnn = xb.shape
    tc = LRU_CHUNK
    ncc, ncl = rows.c // tc, rows.t // tc
    gw = d_rnn // LRU_GROUPS
    pitch = tc + SUBLANES
    per8 = tc // SUBLANES
    last8 = n_seq // SUBLANES - 1

    def chunk(d, s):
        rev = jnp.where(s < ncc, ncc - 1 - s, ncc + ncl - 1 - (s - ncc))
        return jnp.where(d == 0, s, rev)

    n_slabs = d_rnn // LANES
    return pl.pallas_call(
        functools.partial(_lru_scan_kernel, ncc=ncc, ncl=ncl, pitch=pitch),
        grid=(2, ncc + ncl),
        in_specs=[
            pl.BlockSpec((bsz, tc, d_rnn), lambda d, s: (0, chunk(d, s), 0)),
            pl.BlockSpec((bsz, SUBLANES, d_rnn), lambda d, s: (0, jnp.maximum(chunk(d, s) * per8 - 1, 0), 0)),
            pl.BlockSpec((bsz, SUBLANES, d_rnn),
                         lambda d, s: (0, jnp.minimum((chunk(d, s) + 1) * per8, last8), 0)),
            pl.BlockSpec((CONV_W, d_rnn), lambda d, s: (0, 0)),
            pl.BlockSpec((1, d_rnn), lambda d, s: (0, 0)),
            pl.BlockSpec((1, LRU_GROUPS, gw, 2 * gw), lambda d, s: (d, 0, 0, 0)),
            pl.BlockSpec((1, LRU_GROUPS, 1, 2 * gw), lambda d, s: (d, 0, 0, 0)),
            pl.BlockSpec((1, 1, d_rnn), lambda d, s: (d, 0, 0)),
        ],
        out_specs=pl.BlockSpec((1, bsz, tc, d_rnn), lambda d, s: (d, 0, chunk(d, s), 0)),
        out_shape=jax.ShapeDtypeStruct((2, bsz, n_seq, d_rnn), F32),
        scratch_shapes=[
            pltpu.VMEM((tc + 2 * SUBLANES, d_rnn), F32),
            pltpu.VMEM((n_slabs, bsz * pitch, LANES), F32),
            pltpu.VMEM((n_slabs, bsz * pitch, LANES), F32),
            pltpu.VMEM((n_slabs, bsz, LANES), F32),
        ],
        compiler_params=_params(("arbitrary", "arbitrary")),
        name="lru_scan",
    )(xb, xb, xb, conv_w, conv_b, gate_w, gate_b, lam)


def _lru_gate_weights(gate_w, gate_b):
    n_dir, n_gate, g, blk, _ = gate_w.shape
    per = g // LRU_GROUPS
    gw = per * blk
    w = gate_w.reshape(n_dir, n_gate, LRU_GROUPS, per, blk, blk)
    eye = jnp.eye(per, dtype=gate_w.dtype)
    dense = jnp.einsum('dkgpij,pq->dgpikqj', w, eye).reshape(n_dir, LRU_GROUPS, gw, n_gate * gw)
    bias = gate_b.reshape(n_dir, n_gate, LRU_GROUPS, gw).transpose(0, 2, 1, 3)
    return _mx(dense), bias.reshape(n_dir, LRU_GROUPS, 1, n_gate * gw)


def _lru_out_kernel(hf_ref, hb_ref, gate_ref, x_ref, mod_ref, w_ref, o_ref):
    hsum = hf_ref[0, 0] + hb_ref[0, 0]
    y = _dot(_mx(hsum * gate_ref[...].astype(F32)), w_ref[...])
    o_ref[...] = x_ref[...] + mod_ref[0, 5:6, :] * y


def _lru_out(rows, hs, gate, x, mod, w_out):
    d = x.shape[1]
    d_rnn = gate.shape[1]
    tm = rows.tile(LRU_ROW_TILE)
    seq = rows.seq_of_tile(tm)
    return pl.pallas_call(
        _lru_out_kernel,
        grid=(rows.total // tm,),
        in_specs=[
            pl.BlockSpec((1, 1, tm, d_rnn), lambda i: (0,) + seq(i) + (0,)),
            pl.BlockSpec((1, 1, tm, d_rnn), lambda i: (1,) + seq(i) + (0,)),
            pl.BlockSpec((tm, d_rnn), lambda i: (i, 0)),
            pl.BlockSpec((tm, d), lambda i: (i, 0)),
            pl.BlockSpec((1, N_MOD, d), lambda i, f=rows.mod_row(tm): (f(i), 0, 0)),
            _resident((d_rnn, d), lambda i: (0, 0)),
        ],
        out_specs=pl.BlockSpec((tm, d), lambda i: (i, 0)),
        out_shape=jax.ShapeDtypeStruct(x.shape, F32),
        compiler_params=_params(("parallel",)),
        name="lru_out",
    )(hs, hs, gate, x, mod, w_out)


def _mixer_rglru(rows, x, mod, g, w_in, conv_w, conv_b, gate_w, gate_b, lam, w_out):
    xb, gate = _lru_in(rows, x, mod, g, _mx(w_in))
    gw_dense, gb = _lru_gate_weights(gate_w, gate_b)
    hs = _lru_scan(rows, xb, conv_w, conv_b.reshape(1, -1), gw_dense, gb,
                   lam.reshape(2, 1, -1))
    return _lru_out(rows, hs, gate, x, mod, _mx(w_out))


def _ret_in_kernel(x_ref, mod_ref, g_ref, w_ref, cos_ref, sin_ref, q_ref, k_ref, v_ref, sg_ref,
                   *, hdk, hdv, dk):
    h = _mx(_norm_mod(x_ref[...], g_ref[...], mod_ref[0, 3:4, :], mod_ref[0, 4:5, :]))
    cos, sin = cos_ref[...], sin_ref[...]
    half = dk // 2

    def rope(z):
        parts = []
        for hd in range(hdk // dk):
            z1 = z[:, hd * dk:hd * dk + half]
            z2 = z[:, hd * dk + half:(hd + 1) * dk]
            parts += [z1 * cos - z2 * sin, z1 * sin + z2 * cos]
        return jnp.concatenate(parts, axis=-1)

    q_ref[...] = rope(_dot(h, w_ref[:, :hdk])).astype(q_ref.dtype)
    k_ref[...] = (rope(_dot(h, w_ref[:, hdk:2 * hdk])) * (dk ** -0.5)).astype(k_ref.dtype)
    v_ref[...] = _dot(h, w_ref[:, 2 * hdk:2 * hdk + hdv]).astype(v_ref.dtype)
    sg_ref[...] = _silu(_dot(h, w_ref[:, 2 * hdk + hdv:])).astype(sg_ref.dtype)


def _ret_in(rows, x, mod, g, w_in, cos_t, sin_t):
    d = x.shape[1]
    hdk = d
    hdv = (w_in.shape[1] - 2 * hdk) // 2
    dk = hdk // RET_HEADS
    tm = rows.tile(512)
    pos = rows.pos_block(tm)
    row_spec = lambda n: pl.BlockSpec((tm, n), lambda i: (i, 0))
    return pl.pallas_call(
        functools.partial(_ret_in_kernel, hdk=hdk, hdv=hdv, dk=dk),
        grid=(rows.total // tm,),
        in_specs=[
            row_spec(d),
            pl.BlockSpec((1, N_MOD, d), lambda i, f=rows.mod_row(tm): (f(i), 0, 0)),
            pl.BlockSpec((1, d), lambda i: (0, 0)),
            _resident(w_in.shape, lambda i: (0, 0)),
            pl.BlockSpec((tm, dk // 2), lambda i: (pos(i), 0)),
            pl.BlockSpec((tm, dk // 2), lambda i: (pos(i), 0)),
        ],
        out_specs=[row_spec(hdk), row_spec(hdk), row_spec(hdv), row_spec(hdv)],
        out_shape=[jax.ShapeDtypeStruct((rows.total, n), MXU_DTYPE) for n in (hdk, hdk, hdv, hdv)],
        compiler_params=_params(("parallel",)),
        name="ret_in",
    )(x, mod, g, w_in, cos_t, sin_t)


def _ret_log_gamma(hd):
    return float(np.log1p(-np.exp2(np.float32(-5.0 - hd)), dtype=np.float32))


def _ret_state_step(q, k, v, s_ref, hd, lg, q_pow, k_pow):
    s_old = s_ref[hd]
    o = _dot(_mx(q.astype(F32) * jnp.exp(lg * q_pow)), _mx(s_old))
    kd = (k.astype(F32) * jnp.exp(lg * k_pow)).T
    s_ref[hd] = s_old * math.exp(lg * q.shape[0]) + _dot(_mx(kd), v)
    return o


def _ret_fwd_kernel(q_ref, k_ref, v_ref, o_ref, s_ref, *, dk, dv):
    tc = q_ref.shape[0]

    @pl.when(pl.program_id(1) == 0)
    def _():
        s_ref[...] = jnp.zeros(s_ref.shape, F32)

    row = lax.broadcasted_iota(jnp.int32, (tc, tc), 0)
    col = lax.broadcasted_iota(jnp.int32, (tc, tc), 1)
    dist = jnp.abs(row - col).astype(F32)
    idx = lax.broadcasted_iota(jnp.int32, (tc, 1), 0).astype(F32)
    for hd in range(RET_HEADS):
        lg = _ret_log_gamma(hd)
        q = q_ref[:, hd * dk:(hd + 1) * dk]
        k = k_ref[:, hd * dk:(hd + 1) * dk]
        v = v_ref[:, hd * dv:(hd + 1) * dv]
        scores = _dot_nt(q, k) * jnp.exp(lg * dist)
        o = _dot(_mx(scores), v)
        o = o + _ret_state_step(q, k, v, s_ref, hd, lg, idx + 1.0, (tc - 1.0) - idx)
        o_ref[:, hd * dv:(hd + 1) * dv] = o


def _ret_bwd_kernel(q_ref, k_ref, v_ref, of_ref, sg_ref, x_ref, mod_ref, w_ref, o_ref, s_ref, y_ref,
                    *, dk, dv):
    tc = q_ref.shape[0]

    @pl.when(pl.program_id(1) == 0)
    def _():
        s_ref[...] = jnp.zeros(s_ref.shape, F32)

    idx = lax.broadcasted_iota(jnp.int32, (tc, 1), 0).astype(F32)
    for hd in range(RET_HEADS):
        lg = _ret_log_gamma(hd)
        q = q_ref[:, hd * dk:(hd + 1) * dk]
        k = k_ref[:, hd * dk:(hd + 1) * dk]
        v = v_ref[:, hd * dv:(hd + 1) * dv]
        o = of_ref[:, hd * dv:(hd + 1) * dv] + _ret_state_step(q, k, v, s_ref, hd, lg, tc - idx, idx)
        on = o * lax.rsqrt(jnp.mean(o * o, axis=-1, keepdims=True) + EPS)
        y_ref[:, hd * dv:(hd + 1) * dv] = _mx(on * sg_ref[:, hd * dv:(hd + 1) * dv].astype(F32))
    o_ref[...] = x_ref[...] + mod_ref[0, 5:6, :] * _dot(y_ref[...], w_ref[...])


def _ret_core(rows, q, k, v, sg, x, mod, w_out):
    d = x.shape[1]
    hdk, hdv = q.shape[1], v.shape[1]
    dk, dv = hdk // RET_HEADS, hdv // RET_HEADS
    tc = RET_CHUNK
    n_steps = (rows.c + rows.t) // tc
    ncc = rows.c // tc
    bsz = rows.b
    fwd, bwd = rows.seq_block(tc, False), rows.seq_block(tc, True)
    seq = lambda f, n: pl.BlockSpec((tc, n), lambda b, s: (f(b, s), 0))
    o_part = pl.pallas_call(
        functools.partial(_ret_fwd_kernel, dk=dk, dv=dv),
        grid=(bsz, n_steps),
        in_specs=[seq(fwd, hdk), seq(fwd, hdk), seq(fwd, hdv)],
        out_specs=seq(fwd, hdv),
        out_shape=jax.ShapeDtypeStruct((rows.total, hdv), F32),
        scratch_shapes=[pltpu.VMEM((RET_HEADS, dk, dv), F32)],
        compiler_params=_params(("parallel", "arbitrary")),
        name="ret_fwd",
    )(q, k, v)
    return pl.pallas_call(
        functools.partial(_ret_bwd_kernel, dk=dk, dv=dv),
        grid=(bsz, n_steps),
        in_specs=[seq(bwd, hdk), seq(bwd, hdk), seq(bwd, hdv), seq(bwd, hdv), seq(bwd, hdv), seq(bwd, d),
                  pl.BlockSpec((1, N_MOD, d), lambda b, s: (jnp.where(s < ncc, bsz, b), 0, 0)),
                  _resident((hdv, d), lambda b, s: (0, 0))],
        out_specs=seq(bwd, d),
        out_shape=jax.ShapeDtypeStruct(x.shape, F32),
        scratch_shapes=[pltpu.VMEM((RET_HEADS, dk, dv), F32), pltpu.VMEM((tc, hdv), MXU_DTYPE)],
        compiler_params=_params(("parallel", "arbitrary")),
        name="ret_bwd",
    )(q, k, v, o_part, sg, x, mod, w_out)


def _retention_tables(rows, tm, dk):
    theta = RET_THETA_BASE ** (-jnp.linspace(0.0, 1.0, dk // 2, dtype=F32))
    ang = jnp.arange(rows.t, dtype=F32)[:, None] * theta
    ones, zeros = jnp.ones((tm, dk // 2), F32), jnp.zeros((tm, dk // 2), F32)
    return jnp.concatenate([ones, jnp.cos(ang)]), jnp.concatenate([zeros, jnp.sin(ang)])


def _mixer_retention(rows, x, mod, g, w_in, w_out):
    tm = rows.tile(512)
    cos_t, sin_t = _retention_tables(rows, tm, x.shape[1] // RET_HEADS)
    q, k, v, sg = _ret_in(rows, x, mod, g, _mx(w_in), cos_t, sin_t)
    return _ret_core(rows, q, k, v, sg, x, mod, _mx(w_out))


def _dif_in_kernel(x_ref, mod_ref, g_ref, w_ref, cos_ref, sin_ref, q_ref, k_ref, v_ref, *, dh):
    d = x_ref.shape[1]
    h = _mx(_norm_mod(x_ref[...], g_ref[...], mod_ref[0, 3:4, :], mod_ref[0, 4:5, :]))
    cos, sin = cos_ref[...], sin_ref[...]
    lane = lax.broadcasted_iota(jnp.int32, cos.shape, 1)
    first_half = (lane % dh) < (dh // 2)

    def rope(z):
        parts = []
        for j in range(d // LANES):
            zj = z[:, j * LANES:(j + 1) * LANES]
            swapped = jnp.where(first_half, pltpu.roll(zj, LANES - dh // 2, 1), pltpu.roll(zj, dh // 2, 1))
            parts.append(zj * cos + swapped * sin)
        return jnp.concatenate(parts, axis=-1)

    q_ref[...] = (rope(_dot(h, w_ref[:, :d])) * (dh ** -0.5)).astype(q_ref.dtype)
    k_ref[...] = rope(_dot(h, w_ref[:, d:2 * d])).astype(k_ref.dtype)
    v_ref[...] = _dot(h, w_ref[:, 2 * d:]).astype(v_ref.dtype)


def _dif_in(rows, x, mod, g, w_in, cos_t, sin_t, dh):
    d = x.shape[1]
    tm = rows.tile(512)
    pos = rows.pos_block(tm)
    row_spec = pl.BlockSpec((tm, d), lambda i: (i, 0))
    return pl.pallas_call(
        functools.partial(_dif_in_kernel, dh=dh),
        grid=(rows.total // tm,),
        in_specs=[
            row_spec,
            pl.BlockSpec((1, N_MOD, d), lambda i, f=rows.mod_row(tm): (f(i), 0, 0)),
            pl.BlockSpec((1, d), lambda i: (0, 0)),
            _resident(w_in.shape, lambda i: (0, 0)),
            pl.BlockSpec((tm, LANES), lambda i: (pos(i), 0)),
            pl.BlockSpec((tm, LANES), lambda i: (pos(i), 0)),
        ],
        out_specs=[row_spec] * 3,
        out_shape=[jax.ShapeDtypeStruct((rows.total, d), MXU_DTYPE)] * 3,
        compiler_params=_params(("parallel",)),
        name="dif_in",
    )(x, mod, g, w_in, cos_t, sin_t)


def _dif_attn_kernel(q_ref, k_ref, v_ref, x_ref, mod_ref, lam_ref, sub_ref, w_ref, o_ref,
                     m_ref, l_ref, acc_ref, y_ref, *, dh, lam_init):
    j = pl.program_id(2)
    tq = q_ref.shape[0]
    dv = 2 * dh

    @pl.when(j == 0)
    def _():
        m_ref[...] = jnp.full(m_ref.shape, -jnp.inf, F32)
        l_ref[...] = jnp.zeros(l_ref.shape, F32)
        acc_ref[...] = jnp.zeros(acc_ref.shape, F32)

    lane = lax.broadcasted_iota(jnp.int32, (tq, dv), 1)
    for hd in range(DIFF_HEADS):
        q = q_ref[:, hd * dv:(hd + 1) * dv]
        k = k_ref[:, hd * dv:(hd + 1) * dv]
        v = v_ref[:, hd * dv:(hd + 1) * dv]
        for mp in range(2):
            qm = jnp.where((lane >= mp * dh) & (lane < (mp + 1) * dh), q, jnp.zeros_like(q))
            s = _dot_nt(qm, k)
            col = 2 * hd + mp
            m_old = m_ref[col]
            m_new = jnp.maximum(m_old, jnp.max(s, axis=-1, keepdims=True))
            alpha = jnp.exp(m_old - m_new)
            p = jnp.exp(s - m_new)
            l_ref[col] = alpha * l_ref[col] + jnp.sum(p, axis=-1, keepdims=True)
            acc_ref[col] = alpha * acc_ref[col] + _dot(_mx(p), v)
            m_ref[col] = m_new

    @pl.when(j == pl.num_programs(2) - 1)
    def _():
        lp = lam_ref[...]
        lam = (jnp.exp(jnp.sum(lp[0:1] * lp[1:2], axis=-1, keepdims=True))
               - jnp.exp(jnp.sum(lp[2:3] * lp[3:4], axis=-1, keepdims=True)) + lam_init)
        for hd in range(DIFF_HEADS):
            o = acc_ref[2 * hd] / l_ref[2 * hd] - lam * (acc_ref[2 * hd + 1] / l_ref[2 * hd + 1])
            on = o * lax.rsqrt(jnp.mean(o * o, axis=-1, keepdims=True) + EPS) * sub_ref[...]
            y_ref[:, hd * dv:(hd + 1) * dv] = _mx(on * (1.0 - lam_init))
        o_ref[...] = x_ref[...] + mod_ref[0, 5:6, :] * _dot(y_ref[...], w_ref[...])


def _dif_attn(rows, q, k, v, x, mod, lam_p, subln, w_out, lam_init, dh, ctx_queries):
    d = x.shape[1]
    bsz = rows.b
    tk = math.gcd(ATT_TK, rows.c)
    ncc, ncl = rows.c // tk, rows.t // tk
    if ctx_queries:
        tq = math.gcd(ATT_TQ, rows.c)
        nq, n_kv, q0, mod_row = rows.c // tq, ncc, 0, (lambda b: bsz)
    else:
        tq = math.gcd(ATT_TQ, rows.t)
        nq, n_kv, q0, mod_row = rows.t // tq, ncc + ncl, bsz * rows.c // tq, (lambda b: b)
    kv_block = rows.seq_block(tk, False)
    q_spec = pl.BlockSpec((tq, d), lambda b, i, j: (q0 + b * nq + i, 0))
    kv_spec = pl.BlockSpec((tk, d), lambda b, i, j: (kv_block(b, j), 0))
    return pl.pallas_call(
        functools.partial(_dif_attn_kernel, dh=dh, lam_init=lam_init),
        grid=(bsz, nq, n_kv),
        in_specs=[
            q_spec, kv_spec, kv_spec, q_spec,
            pl.BlockSpec((1, N_MOD, d), lambda b, i, j: (mod_row(b), 0, 0)),
            pl.BlockSpec(lam_p.shape, lambda b, i, j: (0, 0)),
            pl.BlockSpec(subln.shape, lambda b, i, j: (0, 0)),
            _resident(w_out.shape, lambda b, i, j: (0, 0)),
        ],
        out_specs=q_spec,
        out_shape=jax.ShapeDtypeStruct(x.shape, F32),
        input_output_aliases={3: 0},
        scratch_shapes=[
            pltpu.VMEM((2 * DIFF_HEADS, tq, 1), F32),
            pltpu.VMEM((2 * DIFF_HEADS, tq, 1), F32),
            pltpu.VMEM((2 * DIFF_HEADS, tq, 2 * dh), F32),
            pltpu.VMEM((tq, d), MXU_DTYPE),
        ],
        compiler_params=_params(("parallel", "parallel", "arbitrary")),
        name="dif_attn_ctx" if ctx_queries else "dif_attn",
    )(q, k, v, x, mod, lam_p, subln, w_out)


def _diff_tables(rows, tm, dh):
    n_freq = dh // 4
    inv = ROPE_BASE ** (-jnp.arange(n_freq, dtype=F32) / n_freq)
    n_rows = rows.t // GRID_W
    row = jnp.broadcast_to(jnp.arange(n_rows, dtype=F32)[:, None], (n_rows, GRID_W)).reshape(-1)
    col = jnp.broadcast_to(jnp.arange(GRID_W, dtype=F32)[None, :], (n_rows, GRID_W)).reshape(-1)
    ang = jnp.concatenate([row[:, None] * inv, col[:, None] * inv], axis=-1)
    cos, sin = jnp.cos(ang), jnp.sin(ang)
    reps = LANES // dh
    cos_l = jnp.tile(jnp.concatenate([cos, cos], axis=-1), (1, reps))
    sin_l = jnp.tile(jnp.concatenate([-sin, sin], axis=-1), (1, reps))
    ones, zeros = jnp.ones((tm, LANES), F32), jnp.zeros((tm, LANES), F32)
    return jnp.concatenate([ones, cos_l]), jnp.concatenate([zeros, sin_l])


def _mixer_diff(rows, x, mod, g, w_in, lam_p, subln, w_out, lam_init, need_ctx_out):
    d = x.shape[1]
    dh = d // DIFF_HEADS // 2
    tm = rows.tile(512)
    cos_t, sin_t = _diff_tables(rows, tm, dh)
    q, k, v = _dif_in(rows, x, mod, g, _mx(w_in), cos_t, sin_t, dh)
    args = (mod, lam_p, subln.reshape(1, -1), _mx(w_out), lam_init, dh)
    x = _dif_attn(rows, q, k, v, x, *args, ctx_queries=False)
    if need_ctx_out:
        x = _dif_attn(rows, q, k, v, x, *args, ctx_queries=True)
    return x


def _final_kernel(x_ref, g_ref, o_ref):
    x = x_ref[...]
    o_ref[...] = x * lax.rsqrt(jnp.mean(x * x, axis=-1, keepdims=True) + EPS) * g_ref[...]


def _final_norm(rows, x, g):
    d = x.shape[1]
    tm = rows.tile(1024)
    n_ctx_tiles = rows.b * rows.c // tm
    return pl.pallas_call(
        _final_kernel,
        grid=(rows.b * rows.t // tm,),
        in_specs=[pl.BlockSpec((tm, d), lambda i: (i + n_ctx_tiles, 0)),
                  pl.BlockSpec((1, d), lambda i: (0, 0))],
        out_specs=pl.BlockSpec((tm, d), lambda i: (i, 0)),
        out_shape=jax.ShapeDtypeStruct((rows.b * rows.t, d), F32),
        compiler_params=_params(("parallel",)),
        name="final_norm",
    )(x, g)


def kernel(x, c, ctx, c_ctx, norm_g, mod_w, mod_b, ffn_w_in, ffn_w_out, lru_w_in, lru_conv_w, lru_conv_b,
           lru_gate_w, lru_gate_b, lru_lam, lru_w_out, ret_w_in, ret_w_out, dif_w_in, dif_lam, dif_subln,
           dif_w_out, final_g):
    bsz, n_lat, d = x.shape
    n_ctx = ctx.shape[1]
    depth = norm_g.shape[0]
    assert bsz + 1 <= SUBLANES
    rows = _Rows(bsz, n_ctx, n_lat)

    cond = jnp.zeros((SUBLANES, d), F32).at[:bsz].set(c).at[bsz].set(c_ctx)
    mods = _ada_mod(cond, mod_w, mod_b).reshape(depth, SUBLANES, N_MOD, d)
    xs = jnp.concatenate([ctx.reshape(bsz * n_ctx, d), x.reshape(bsz * n_lat, d)], axis=0)

    fc = 256
    for i in range(depth):
        last = i == depth - 1
        mod = mods[i]
        g = norm_g[i].reshape(3, 1, d)
        xs = _ffn(rows, xs, mod, g[0], *_ffn_weights(ffn_w_in[i, 0], ffn_w_out[i, 0], fc), k0=0)
        kind, j = i % 3, i // 3
        if kind == 0:
            xs = _mixer_rglru(rows, xs, mod, g[1], lru_w_in[j], lru_conv_w[j], lru_conv_b[j], lru_gate_w[j],
                              lru_gate_b[j], lru_lam[j], lru_w_out[j])
        elif kind == 1:
            xs = _mixer_retention(rows, xs, mod, g[1], ret_w_in[j], ret_w_out[j])
        else:
            lam_init = 0.8 - 0.6 * math.exp(-0.3 * i)
            xs = _mixer_diff(rows, xs, mod, g[1], dif_w_in[j], dif_lam[j], dif_subln[j], dif_w_out[j],
                             lam_init, not last)
        xs = _ffn(rows, xs, mod, g[2], *_ffn_weights(ffn_w_in[i, 1], ffn_w_out[i, 1], fc), k0=6)
    return _final_norm(rows, xs, final_g.reshape(1, d)).reshape(bsz, n_lat, d)
```

```python
import functools
import math

import numpy as np
import jax
import jax.numpy as jnp
from jax import lax
from jax.experimental import pallas as pl
from jax.experimental.pallas import tpu as pltpu

F32 = jnp.float32
MXU_DTYPE = jnp.bfloat16

EPS = 1e-6
N_MOD = 9
MACARON_W = 0.5
GRID_W = 64
ROPE_BASE = 10000.0
LRU_BLOCKS = 8
LRU_GROUPS = 2
CONV_W = 4
CONV_PAD_LEFT = 1
LRU_C = 8.0
LRU_CHUNK = 128
LRU_ROW_TILE = 256
RET_HEADS = 4
RET_CHUNK = 128
RET_THETA_BASE = 10000.0
DIFF_HEADS = 8
ATT_TQ = 512
ATT_QS = 256
ATT_TK = 1408
ATT_ROW_TILE = 256

LANES = 128
SUBLANES = 8
VMEM_LIMIT = 56 * 1024 * 1024


def _params(sem):
    return pltpu.CompilerParams(dimension_semantics=sem, vmem_limit_bytes=VMEM_LIMIT)


def _resident(shape, index_map):
    return pl.BlockSpec(shape, index_map, pipeline_mode=pl.Buffered(1))


def _mx(v):
    return v.astype(MXU_DTYPE)


def _dot(a, b):
    return jnp.dot(a, b, preferred_element_type=F32)


def _dot_nt(a, b):
    return lax.dot_general(a, b, (((1,), (1,)), ((), ())), preferred_element_type=F32)


def _norm_mod(x, g, shift, scale):
    ms = jnp.mean(x * x, axis=-1, keepdims=True)
    return (x * lax.rsqrt(ms + EPS)) * g * (1.0 + scale) + shift


def _silu(v):
    return v * jax.nn.sigmoid(v)


class _Rows:
    def __init__(self, bsz, n_ctx, n_lat):
        self.b, self.c, self.t = bsz, n_ctx, n_lat
        self.total = bsz * (n_ctx + n_lat)

    def tile(self, cap):
        return math.gcd(math.gcd(self.b * self.c, self.t), cap)

    def mod_row(self, tm):
        n_ctx_tiles = self.b * self.c // tm
        per_batch = self.t // tm
        bsz = self.b
        return lambda i: jnp.where(i < n_ctx_tiles, bsz, (i - n_ctx_tiles) // per_batch)

    def pos_block(self, tm):
        n_ctx_tiles = self.b * self.c // tm
        per_batch = self.t // tm
        return lambda i: jnp.where(i < n_ctx_tiles, 0, 1 + (i - n_ctx_tiles) % per_batch)

    def seq_of_tile(self, tm):
        n_ctx_tiles = self.b * self.c // tm
        ctx_per, lat_per = self.c // tm, self.t // tm

        def f(i):
            r = i - n_ctx_tiles
            is_ctx = i < n_ctx_tiles
            return (jnp.where(is_ctx, i // ctx_per, r // lat_per),
                    jnp.where(is_ctx, i % ctx_per, ctx_per + r % lat_per))
        return f

    def seq_block(self, chunk, reverse):
        ncc, ncl = self.c // chunk, self.t // chunk
        bsz = self.b

        def f(b, s):
            if reverse:
                q_ctx, q_lat = ncc - 1 - s, ncl - 1 - (s - ncc)
            else:
                q_ctx, q_lat = s, s - ncc
            return jnp.where(s < ncc, b * ncc + q_ctx, bsz * ncc + b * ncl + q_lat)
        return f


def _ada_kernel(c_ref, w_ref, b_ref, o_ref):
    s = _mx(_silu(c_ref[...]))
    o_ref[0] = _dot(s, _mx(w_ref[0])) + b_ref[0]


def _ada_mod(cond, mod_w, mod_b):
    depth, d, n = mod_w.shape
    tn = 1024
    return pl.pallas_call(
        _ada_kernel,
        grid=(depth, n // tn),
        in_specs=[
            pl.BlockSpec((SUBLANES, d), lambda l, j: (0, 0)),
            pl.BlockSpec((1, d, tn), lambda l, j: (l, 0, j)),
            pl.BlockSpec((1, 1, tn), lambda l, j: (l, 0, j)),
        ],
        out_specs=pl.BlockSpec((1, SUBLANES, tn), lambda l, j: (l, 0, j)),
        out_shape=jax.ShapeDtypeStruct((depth, SUBLANES, n), F32),
        compiler_params=_params(("parallel", "parallel")),
        name="ada_mod",
    )(cond, mod_w, mod_b.reshape(depth, 1, n))


def _ffn_kernel(x_ref, mod_ref, g_ref, win_ref, wout_ref, o_ref, *, k0, n_chunks):
    x = x_ref[...]
    h = _mx(_norm_mod(x, g_ref[...], mod_ref[0, k0:k0 + 1, :], mod_ref[0, k0 + 1:k0 + 2, :]))
    acc = jnp.zeros(x.shape, F32)
    for c in range(n_chunks):
        a = _dot(h, win_ref[0, c])
        g = _dot(h, win_ref[1, c])
        acc = acc + _dot(_mx(_silu(a) * g), wout_ref[c])
    o_ref[...] = x + (MACARON_W * mod_ref[0, k0 + 2:k0 + 3, :]) * acc


def _ffn(rows, x, mod, g, w_in, w_out, k0):
    d = x.shape[1]
    n_chunks, fc = w_out.shape[0], w_out.shape[1]
    tm = rows.tile(512)
    return pl.pallas_call(
        functools.partial(_ffn_kernel, k0=k0, n_chunks=n_chunks),
        grid=(rows.total // tm,),
        in_specs=[
            pl.BlockSpec((tm, d), lambda i: (i, 0)),
            pl.BlockSpec((1, N_MOD, d), lambda i, f=rows.mod_row(tm): (f(i), 0, 0)),
            pl.BlockSpec((1, d), lambda i: (0, 0)),
            _resident((2, n_chunks, d, fc), lambda i: (0, 0, 0, 0)),
            _resident((n_chunks, fc, d), lambda i: (0, 0, 0)),
        ],
        out_specs=pl.BlockSpec((tm, d), lambda i: (i, 0)),
        out_shape=jax.ShapeDtypeStruct(x.shape, F32),
        compiler_params=_params(("parallel",)),
        name="ffn",
    )(x, mod, g, w_in, w_out)


def _ffn_weights(w_in, w_out, fc):
    d, two_f = w_in.shape
    f = two_f // 2
    w_in = _mx(w_in).reshape(d, 2, f // fc, fc).transpose(1, 2, 0, 3)
    return w_in, _mx(w_out).reshape(f // fc, fc, d)


def _lru_in_kernel(x_ref, mod_ref, g_ref, w_ref, xb_ref, gate_ref, *, d_rnn):
    h = _mx(_norm_mod(x_ref[...], g_ref[...], mod_ref[0, 3:4, :], mod_ref[0, 4:5, :]))
    xb_ref[0] = _dot(h, w_ref[:, :d_rnn])
    gate_ref[...] = jax.nn.gelu(_dot(h, w_ref[:, d_rnn:])).astype(gate_ref.dtype)


def _lru_in(rows, x, mod, g, w_in):
    d = x.shape[1]
    d_rnn = w_in.shape[1] // 2
    tm = rows.tile(LRU_ROW_TILE)
    seq = rows.seq_of_tile(tm)
    return pl.pallas_call(
        functools.partial(_lru_in_kernel, d_rnn=d_rnn),
        grid=(rows.total // tm,),
        in_specs=[
            pl.BlockSpec((tm, d), lambda i: (i, 0)),
            pl.BlockSpec((1, N_MOD, d), lambda i, f=rows.mod_row(tm): (f(i), 0, 0)),
            pl.BlockSpec((1, d), lambda i: (0, 0)),
            _resident((d, 2 * d_rnn), lambda i: (0, 0)),
        ],
        out_specs=[pl.BlockSpec((1, tm, d_rnn), lambda i: seq(i) + (0,)),
                   pl.BlockSpec((tm, d_rnn), lambda i: (i, 0))],
        out_shape=[jax.ShapeDtypeStruct((rows.b, rows.c + rows.t, d_rnn), F32),
                   jax.ShapeDtypeStruct((rows.total, d_rnn), MXU_DTYPE)],
        compiler_params=_params(("parallel",)),
        name="lru_in",
    )(x, mod, g, w_in)


def _lru_scan_kernel(main_ref, prev_ref, next_ref, cw_ref, cb_ref, gw_ref, gb_ref, lam_ref, out_ref,
                     ext_ref, a_ref, b_ref, h_ref, *, ncc, ncl, pitch):
    bsz, tc, d_rnn = main_ref.shape
    d = pl.program_id(0)
    s = pl.program_id(1)
    n_slabs = d_rnn // LANES
    gw = d_rnn // LRU_GROUPS

    @pl.when(s == 0)
    def _():
        h_ref[...] = jnp.zeros(h_ref.shape, F32)

    q_rev = jnp.where(s < ncc, ncc - 1 - s, ncl - 1 - (s - ncc))
    q_fwd = jnp.where(s < ncc, s, s - ncc)
    q = jnp.where(d == 0, q_fwd, q_rev)
    n_seg = jnp.where(s < ncc, ncc, ncl)
    has_prev = (q > 0).astype(F32)
    has_next = (q < n_seg - 1).astype(F32)

    nl = -lam_ref[0]
    sp = jnp.maximum(nl, 0.0) + jnp.log1p(jnp.exp(-jnp.abs(nl)))
    half_rate = (-0.5 * LRU_C * math.log2(math.e)) * sp

    half = tc // 2
    base = SUBLANES - CONV_PAD_LEFT
    def conv(b, slab):
        ls = slice(slab * LANES, (slab + 1) * LANES)
        ext_ref[b, slab, 0:SUBLANES, :] = prev_ref[b, :, ls] * has_prev
        ext_ref[b, slab, SUBLANES:SUBLANES + tc, :] = main_ref[b, :, ls]
        ext_ref[b, slab, SUBLANES + tc:, :] = next_ref[b, :, ls] * has_next
        parities = []
        for par in range(2):
            acc = cb_ref[:, ls] + cw_ref[0:1, ls] * ext_ref[b, slab, pl.ds(base + par, half, stride=2), :]
            for k in range(1, CONV_W):
                acc = acc + cw_ref[k:k + 1, ls] * ext_ref[b, slab, pl.ds(base + par + k, half, stride=2), :]
            parities.append(acc)
        return jnp.concatenate(parities, axis=0)

    per_group = gw // LANES
    for b in range(bsz):
        for gi in range(LRU_GROUPS):
            xg = jnp.concatenate([conv(b, gi * per_group + j) for j in range(per_group)], axis=1)
            t = jnp.tanh(_dot(_mx(xg), gw_ref[0, gi]) + gb_ref[0, gi])
            rate = half_rate[:, gi * gw:(gi + 1) * gw]
            a = jnp.exp2(rate * t[:, :gw] + rate)
            root = jnp.exp2(0.5 * jnp.log2(1.0 - a * a))
            bb = root * ((0.5 * xg) * (t[:, gw:] + 1.0))
            for j in range(per_group):
                ls = slice(j * LANES, (j + 1) * LANES)
                for par in range(2):
                    rows_p = slice(par * half, (par + 1) * half)
                    dst = (gi * per_group + j, pl.ds(b * pitch + par, half, stride=2), slice(None))
                    a_ref[dst] = a[rows_p, ls]
                    b_ref[dst] = bb[rows_p, ls]

    def step(t, hs):
        row = t + d * (tc - 1 - 2 * t)
        new = []
        for slab in range(n_slabs):
            idx = (slab, pl.ds(row, bsz, stride=pitch), slice(None))
            h = a_ref[idx] * hs[slab] + b_ref[idx]
            b_ref[idx] = h
            new.append(h)
        return tuple(new)

    hs = lax.fori_loop(0, tc, step, tuple(h_ref[slab] for slab in range(n_slabs)), unroll=8)
    for slab in range(n_slabs):
        h_ref[slab] = hs[slab]

    for b in range(bsz):
        for slab in range(n_slabs):
            out_ref[0, b, :, slab * LANES:(slab + 1) * LANES] = b_ref[slab, b * pitch:b * pitch + tc, :]


def _lru_scan(rows, xb, conv_w, conv_b, gate_w, gate_b, lam):
    bsz, n_seq, d_rnn = xb.shape
    tc = LRU_CHUNK
    ncc, ncl = rows.c // tc, rows.t // tc
    gw = d_rnn // LRU_GROUPS
    pitch = tc + SUBLANES
    per8 = tc // SUBLANES
    last8 = n_seq // SUBLANES - 1

    def chunk(d, s):
        rev = jnp.where(s < ncc, ncc - 1 - s, ncc + ncl - 1 - (s - ncc))
        return jnp.where(d == 0, s, rev)

    n_slabs = d_rnn // LANES
    return pl.pallas_call(
        functools.partial(_lru_scan_kernel, ncc=ncc, ncl=ncl, pitch=pitch),
        grid=(2, ncc + ncl),
        in_specs=[
            pl.BlockSpec((bsz, tc, d_rnn), lambda d, s: (0, chunk(d, s), 0)),
            pl.BlockSpec((bsz, SUBLANES, d_rnn), lambda d, s: (0, jnp.maximum(chunk(d, s) * per8 - 1, 0), 0)),
            pl.BlockSpec((bsz, SUBLANES, d_rnn),
                         lambda d, s: (0, jnp.minimum((chunk(d, s) + 1) * per8, last8), 0)),
            pl.BlockSpec((CONV_W, d_rnn), lambda d, s: (0, 0)),
            pl.BlockSpec((1, d_rnn), lambda d, s: (0, 0)),
            pl.BlockSpec((1, LRU_GROUPS, gw, 2 * gw), lambda d, s: (d, 0, 0, 0)),
            pl.BlockSpec((1, LRU_GROUPS, 1, 2 * gw), lambda d, s: (d, 0, 0, 0)),
            pl.BlockSpec((1, 1, d_rnn), lambda d, s: (d, 0, 0)),
        ],
        out_specs=pl.BlockSpec((1, bsz, tc, d_rnn), lambda d, s: (d, 0, chunk(d, s), 0)),
        out_shape=jax.ShapeDtypeStruct((2, bsz, n_seq, d_rnn), F32),
        scratch_shapes=[
            pltpu.VMEM((bsz, n_slabs, tc + 2 * SUBLANES, LANES), F32),
            pltpu.VMEM((n_slabs, bsz * pitch, LANES), F32),
            pltpu.VMEM((n_slabs, bsz * pitch, LANES), F32),
            pltpu.VMEM((n_slabs, bsz, LANES), F32),
        ],
        compiler_params=_params(("arbitrary", "arbitrary")),
        name="lru_scan",
    )(xb, xb, xb, conv_w, conv_b, gate_w, gate_b, lam)


def _lru_gate_weights(gate_w, gate_b):
    n_dir, n_gate, g, blk, _ = gate_w.shape
    per = g // LRU_GROUPS
    gw = per * blk
    w = gate_w.reshape(n_dir, n_gate, LRU_GROUPS, per, blk, blk)
    eye = jnp.eye(per, dtype=gate_w.dtype)
    dense = jnp.einsum('dkgpij,pq->dgpikqj', w, eye).reshape(n_dir, LRU_GROUPS, gw, n_gate * gw)
    bias = gate_b.reshape(n_dir, n_gate, LRU_GROUPS, gw).transpose(0, 2, 1, 3)
    return _mx(0.5 * dense), 0.5 * bias.reshape(n_dir, LRU_GROUPS, 1, n_gate * gw)


def _lru_out_kernel(hf_ref, hb_ref, gate_ref, x_ref, mod_ref, w_ref, o_ref):
    hsum = hf_ref[0, 0] + hb_ref[0, 0]
    y = _dot(_mx(hsum * gate_ref[...].astype(F32)), w_ref[...])
    o_ref[...] = x_ref[...] + mod_ref[0, 5:6, :] * y


def _lru_out(rows, hs, gate, x, mod, w_out):
    d = x.shape[1]
    d_rnn = gate.shape[1]
    tm = rows.tile(LRU_ROW_TILE)
    seq = rows.seq_of_tile(tm)
    return pl.pallas_call(
        _lru_out_kernel,
        grid=(rows.total // tm,),
        in_specs=[
            pl.BlockSpec((1, 1, tm, d_rnn), lambda i: (0,) + seq(i) + (0,)),
            pl.BlockSpec((1, 1, tm, d_rnn), lambda i: (1,) + seq(i) + (0,)),
            pl.BlockSpec((tm, d_rnn), lambda i: (i, 0)),
            pl.BlockSpec((tm, d), lambda i: (i, 0)),
            pl.BlockSpec((1, N_MOD, d), lambda i, f=rows.mod_row(tm): (f(i), 0, 0)),
            _resident((d_rnn, d), lambda i: (0, 0)),
        ],
        out_specs=pl.BlockSpec((tm, d), lambda i: (i, 0)),
        out_shape=jax.ShapeDtypeStruct(x.shape, F32),
        compiler_params=_params(("parallel",)),
        name="lru_out",
    )(hs, hs, gate, x, mod, w_out)


def _mixer_rglru(rows, x, mod, g, w_in, conv_w, conv_b, gate_w, gate_b, lam, w_out):
    xb, gate = _lru_in(rows, x, mod, g, _mx(w_in))
    gw_dense, gb = _lru_gate_weights(gate_w, gate_b)
    hs = _lru_scan(rows, xb, conv_w, conv_b.reshape(1, -1), gw_dense, gb,
                   lam.reshape(2, 1, -1))
    return _lru_out(rows, hs, gate, x, mod, _mx(w_out))


def _ret_in_kernel(x_ref, mod_ref, g_ref, w_ref, cos_ref, sin_ref, q_ref, k_ref, v_ref, sg_ref,
                   *, hdk, hdv, dk):
    h = _mx(_norm_mod(x_ref[...], g_ref[...], mod_ref[0, 3:4, :], mod_ref[0, 4:5, :]))
    cos, sin = cos_ref[...], sin_ref[...]
    half = dk // 2

    def rope(z):
        parts = []
        for hd in range(hdk // dk):
            z1 = z[:, hd * dk:hd * dk + half]
            z2 = z[:, hd * dk + half:(hd + 1) * dk]
            parts += [z1 * cos - z2 * sin, z1 * sin + z2 * cos]
        return jnp.concatenate(parts, axis=-1)

    q_ref[...] = rope(_dot(h, w_ref[:, :hdk])).astype(q_ref.dtype)
    k_ref[...] = (rope(_dot(h, w_ref[:, hdk:2 * hdk])) * (dk ** -0.5)).astype(k_ref.dtype)
    v_ref[...] = _dot(h, w_ref[:, 2 * hdk:2 * hdk + hdv]).astype(v_ref.dtype)
    sg_ref[...] = _silu(_dot(h, w_ref[:, 2 * hdk + hdv:])).astype(sg_ref.dtype)


def _ret_in(rows, x, mod, g, w_in, cos_t, sin_t):
    d = x.shape[1]
    hdk = d
    hdv = (w_in.shape[1] - 2 * hdk) // 2
    dk = hdk // RET_HEADS
    tm = rows.tile(512)
    pos = rows.pos_block(tm)
    row_spec = lambda n: pl.BlockSpec((tm, n), lambda i: (i, 0))
    return pl.pallas_call(
        functools.partial(_ret_in_kernel, hdk=hdk, hdv=hdv, dk=dk),
        grid=(rows.total // tm,),
        in_specs=[
            row_spec(d),
            pl.BlockSpec((1, N_MOD, d), lambda i, f=rows.mod_row(tm): (f(i), 0, 0)),
            pl.BlockSpec((1, d), lambda i: (0, 0)),
            _resident(w_in.shape, lambda i: (0, 0)),
            pl.BlockSpec((tm, dk // 2), lambda i: (pos(i), 0)),
            pl.BlockSpec((tm, dk // 2), lambda i: (pos(i), 0)),
        ],
        out_specs=[row_spec(hdk), row_spec(hdk), row_spec(hdv), row_spec(hdv)],
        out_shape=[jax.ShapeDtypeStruct((rows.total, n), MXU_DTYPE) for n in (hdk, hdk, hdv, hdv)],
        compiler_params=_params(("parallel",)),
        name="ret_in",
    )(x, mod, g, w_in, cos_t, sin_t)


def _ret_log_gamma(hd):
    return float(np.log1p(-np.exp2(np.float32(-5.0 - hd)), dtype=np.float32))


def _ret_state_step(q, k, v, s_ref, hd, lg, q_pow, k_pow):
    s_old = s_ref[hd]
    o = _dot(_mx(q.astype(F32) * jnp.exp(lg * q_pow)), _mx(s_old))
    kd = (k.astype(F32) * jnp.exp(lg * k_pow)).T
    s_ref[hd] = s_old * math.exp(lg * q.shape[0]) + _dot(_mx(kd), v)
    return o


def _ret_fwd_kernel(q_ref, k_ref, v_ref, o_ref, s_ref, *, dk, dv):
    tc = q_ref.shape[0]

    @pl.when(pl.program_id(1) == 0)
    def _():
        s_ref[...] = jnp.zeros(s_ref.shape, F32)

    row = lax.broadcasted_iota(jnp.int32, (tc, tc), 0)
    col = lax.broadcasted_iota(jnp.int32, (tc, tc), 1)
    dist = jnp.abs(row - col).astype(F32)
    idx = lax.broadcasted_iota(jnp.int32, (tc, 1), 0).astype(F32)
    for hd in range(RET_HEADS):
        lg = _ret_log_gamma(hd)
        q = q_ref[:, hd * dk:(hd + 1) * dk]
        k = k_ref[:, hd * dk:(hd + 1) * dk]
        v = v_ref[:, hd * dv:(hd + 1) * dv]
        scores = _dot_nt(q, k) * jnp.exp(lg * dist)
        o = _dot(_mx(scores), v)
        o = o + _ret_state_step(q, k, v, s_ref, hd, lg, idx + 1.0, (tc - 1.0) - idx)
        o_ref[:, hd * dv:(hd + 1) * dv] = o


def _ret_bwd_kernel(q_ref, k_ref, v_ref, of_ref, sg_ref, x_ref, mod_ref, w_ref, o_ref, s_ref, y_ref,
                    *, dk, dv):
    tc = q_ref.shape[0]

    @pl.when(pl.program_id(1) == 0)
    def _():
        s_ref[...] = jnp.zeros(s_ref.shape, F32)

    idx = lax.broadcasted_iota(jnp.int32, (tc, 1), 0).astype(F32)
    for hd in range(RET_HEADS):
        lg = _ret_log_gamma(hd)
        q = q_ref[:, hd * dk:(hd + 1) * dk]
        k = k_ref[:, hd * dk:(hd + 1) * dk]
        v = v_ref[:, hd * dv:(hd + 1) * dv]
        o = of_ref[:, hd * dv:(hd + 1) * dv] + _ret_state_step(q, k, v, s_ref, hd, lg, tc - idx, idx)
        on = o * lax.rsqrt(jnp.mean(o * o, axis=-1, keepdims=True) + EPS)
        y_ref[:, hd * dv:(hd + 1) * dv] = _mx(on * sg_ref[:, hd * dv:(hd + 1) * dv].astype(F32))
    o_ref[...] = x_ref[...] + mod_ref[0, 5:6, :] * _dot(y_ref[...], w_ref[...])


def _ret_core(rows, q, k, v, sg, x, mod, w_out):
    d = x.shape[1]
    hdk, hdv = q.shape[1], v.shape[1]
    dk, dv = hdk // RET_HEADS, hdv // RET_HEADS
    tc = RET_CHUNK
    n_steps = (rows.c + rows.t) // tc
    ncc = rows.c // tc
    bsz = rows.b
    fwd, bwd = rows.seq_block(tc, False), rows.seq_block(tc, True)
    seq = lambda f, n: pl.BlockSpec((tc, n), lambda b, s: (f(b, s), 0))
    o_part = pl.pallas_call(
        functools.partial(_ret_fwd_kernel, dk=dk, dv=dv),
        grid=(bsz, n_steps),
        in_specs=[seq(fwd, hdk), seq(fwd, hdk), seq(fwd, hdv)],
        out_specs=seq(fwd, hdv),
        out_shape=jax.ShapeDtypeStruct((rows.total, hdv), F32),
        scratch_shapes=[pltpu.VMEM((RET_HEADS, dk, dv), F32)],
        compiler_params=_params(("parallel", "arbitrary")),
        name="ret_fwd",
    )(q, k, v)
    return pl.pallas_call(
        functools.partial(_ret_bwd_kernel, dk=dk, dv=dv),
        grid=(bsz, n_steps),
        in_specs=[seq(bwd, hdk), seq(bwd, hdk), seq(bwd, hdv), seq(bwd, hdv), seq(bwd, hdv), seq(bwd, d),
                  pl.BlockSpec((1, N_MOD, d), lambda b, s: (jnp.where(s < ncc, bsz, b), 0, 0)),
                  _resident((hdv, d), lambda b, s: (0, 0))],
        out_specs=seq(bwd, d),
        out_shape=jax.ShapeDtypeStruct(x.shape, F32),
        scratch_shapes=[pltpu.VMEM((RET_HEADS, dk, dv), F32), pltpu.VMEM((tc, hdv), MXU_DTYPE)],
        compiler_params=_params(("parallel", "arbitrary")),
        name="ret_bwd",
    )(q, k, v, o_part, sg, x, mod, w_out)


def _retention_tables(rows, tm, dk):
    theta = RET_THETA_BASE ** (-jnp.linspace(0.0, 1.0, dk // 2, dtype=F32))
    ang = jnp.arange(rows.t, dtype=F32)[:, None] * theta
    ones, zeros = jnp.ones((tm, dk // 2), F32), jnp.zeros((tm, dk // 2), F32)
    return jnp.concatenate([ones, jnp.cos(ang)]), jnp.concatenate([zeros, jnp.sin(ang)])


def _mixer_retention(rows, x, mod, g, w_in, w_out):
    tm = rows.tile(512)
    cos_t, sin_t = _retention_tables(rows, tm, x.shape[1] // RET_HEADS)
    q, k, v, sg = _ret_in(rows, x, mod, g, _mx(w_in), cos_t, sin_t)
    return _ret_core(rows, q, k, v, sg, x, mod, _mx(w_out))


def _dif_in_kernel(x_ref, mod_ref, g_ref, w_ref, cos_ref, sin_ref, q_ref, k_ref, vt_ref, *, dh):
    d = x_ref.shape[1]
    h = _mx(_norm_mod(x_ref[...], g_ref[...], mod_ref[0, 3:4, :], mod_ref[0, 4:5, :]))
    cos, sin = cos_ref[...], sin_ref[...]
    lane = lax.broadcasted_iota(jnp.int32, cos.shape, 1)
    first_half = (lane % dh) < (dh // 2)

    def rope(zj):
        swapped = jnp.where(first_half, pltpu.roll(zj, LANES - dh // 2, 1), pltpu.roll(zj, dh // 2, 1))
        return zj * cos + swapped * sin

    q = _dot(h, w_ref[:, :d])
    k = _dot(h, w_ref[:, d:2 * d])
    v = _dot(h, w_ref[:, 2 * d:])
    first_map = lane < dh
    for hd in range(d // LANES):
        qh = rope(q[:, hd * LANES:(hd + 1) * LANES]) * (dh ** -0.5 * math.log2(math.e))
        q_ref[2 * hd] = jnp.where(first_map, qh, 0.0).astype(q_ref.dtype)
        q_ref[2 * hd + 1] = jnp.where(first_map, 0.0, qh).astype(q_ref.dtype)
        k_ref[0, hd] = rope(k[:, hd * LANES:(hd + 1) * LANES]).astype(k_ref.dtype)
        vt_ref[0, hd] = v[:, hd * LANES:(hd + 1) * LANES].T.astype(vt_ref.dtype)


def _dif_in(rows, x, mod, g, w_in, cos_t, sin_t, dh):
    d = x.shape[1]
    heads = d // LANES
    n_seq = rows.c + rows.t
    tm = rows.tile(ATT_ROW_TILE)
    pos = rows.pos_block(tm)
    seq = rows.seq_of_tile(tm)
    return pl.pallas_call(
        functools.partial(_dif_in_kernel, dh=dh),
        grid=(rows.total // tm,),
        in_specs=[
            pl.BlockSpec((tm, d), lambda i: (i, 0)),
            pl.BlockSpec((1, N_MOD, d), lambda i, f=rows.mod_row(tm): (f(i), 0, 0)),
            pl.BlockSpec((1, d), lambda i: (0, 0)),
            _resident(w_in.shape, lambda i: (0, 0)),
            pl.BlockSpec((tm, LANES), lambda i: (pos(i), 0)),
            pl.BlockSpec((tm, LANES), lambda i: (pos(i), 0)),
        ],
        out_specs=[
            pl.BlockSpec((2 * heads, tm, LANES), lambda i: (0, i, 0)),
            pl.BlockSpec((1, heads, tm, LANES), lambda i: (seq(i)[0], 0, seq(i)[1], 0)),
            pl.BlockSpec((1, heads, LANES, tm), lambda i: (seq(i)[0], 0, 0, seq(i)[1])),
        ],
        out_shape=[
            jax.ShapeDtypeStruct((2 * heads, rows.total, LANES), MXU_DTYPE),
            jax.ShapeDtypeStruct((rows.b, heads, n_seq, LANES), MXU_DTYPE),
            jax.ShapeDtypeStruct((rows.b, heads, LANES, n_seq), MXU_DTYPE),
        ],
        compiler_params=_params(("parallel",)),
        name="dif_in",
    )(x, mod, g, w_in, cos_t, sin_t)


def _dif_attn_kernel(q_ref, k_ref, vt_ref, x_ref, mod_ref, lam_ref, sub_ref, w_ref, o_ref,
                     m_ref, l_ref, acc_ref, y_ref, s0_ref, s1_ref, mc0_ref, mc1_ref, *, dh, lam_init):
    j = pl.program_id(2)
    dv = 2 * dh
    n_heads = k_ref.shape[1]

    @pl.when(j == 0)
    def _():
        m_ref[...] = jnp.full(m_ref.shape, -jnp.inf, F32)
        l_ref[...] = jnp.zeros(l_ref.shape, F32)
        acc_ref[...] = jnp.zeros(acc_ref.shape, F32)

    def scores(hd, col, s_ref, mc_ref):
        st = _dot_nt(k_ref[0, hd], q_ref[col])
        s_ref[...] = st
        mc_ref[...] = jnp.max(st, axis=0, keepdims=True)

    def accumulate(hd, col, s_ref, mc_ref):
        m_old = m_ref[col]
        m_new = jnp.maximum(m_old, mc_ref[...])
        alpha = jnp.exp2(m_old - m_new)
        pt = jnp.exp2(s_ref[...] - m_new)
        l_ref[col] = alpha * l_ref[col] + jnp.sum(pt, axis=0, keepdims=True)
        acc_ref[col] = alpha * acc_ref[col] + _dot(vt_ref[0, hd], _mx(pt))
        m_ref[col] = m_new

    scores(0, 0, s0_ref, mc0_ref)

    def head(hd, carry):
        scores(hd, 2 * hd + 1, s1_ref, mc1_ref)
        accumulate(hd, 2 * hd, s0_ref, mc0_ref)
        nxt = jnp.minimum(hd + 1, n_heads - 1)
        scores(nxt, 2 * nxt, s0_ref, mc0_ref)
        accumulate(hd, 2 * hd + 1, s1_ref, mc1_ref)
        return carry

    lax.fori_loop(0, n_heads, head, 0)

    @pl.when(j == pl.num_programs(2) - 1)
    def _():
        lp = lam_ref[...]
        lam = (jnp.exp(jnp.sum(lp[0:1] * lp[1:2], axis=-1, keepdims=True))
               - jnp.exp(jnp.sum(lp[2:3] * lp[3:4], axis=-1, keepdims=True)) + lam_init)
        for hd in range(n_heads):
            ot =acc_ref[2 * hd] / l_ref[2 * hd] - lam * (acc_ref[2 * hd + 1] / l_ref[2 * hd + 1])
            o = ot.T
            on = o * lax.rsqrt(jnp.mean(o * o, axis=-1, keepdims=True) + EPS) * sub_ref[...]
            y_ref[:, hd * dv:(hd + 1) * dv] = _mx(on * (1.0 - lam_init))
        o_ref[...] = x_ref[...] + mod_ref[0, 5:6, :] * _dot(y_ref[...], w_ref[...])


def _largest_tile(n, cap):
    return max(t for t in range(LANES, min(n, cap) + 1, LANES) if n % t == 0)


def _dif_attn(rows, q, k, v, x, mod, lam_p, subln, w_out, lam_init, dh, ctx_queries):
    d = x.shape[1]
    bsz = rows.b
    if ctx_queries:
        tq = math.gcd(ATT_TQ, rows.c)
        tk = _largest_tile(rows.c, ATT_TK)
        nq, n_kv, q0, mod_row = rows.c // tq, rows.c // tk, 0, (lambda b: bsz)
    else:
        tq = math.gcd(ATT_TQ, rows.t)
        tk = _largest_tile(rows.c + rows.t, ATT_TK)
        nq, n_kv, q0, mod_row = rows.t // tq, (rows.c + rows.t) // tk, bsz * rows.c // tq, (lambda b: b)
    heads, dv = k.shape[1], k.shape[3]
    x_spec = pl.BlockSpec((tq, d), lambda b, i, j: (q0 + b * nq + i, 0))
    return pl.pallas_call(
        functools.partial(_dif_attn_kernel, dh=dh, lam_init=lam_init),
        grid=(bsz, nq, n_kv),
        in_specs=[
            pl.BlockSpec((2 * heads, tq, dv), lambda b, i, j: (0, q0 + b * nq + i, 0)),
            pl.BlockSpec((1, heads, tk, dv), lambda b, i, j: (b, 0, j, 0)),
            pl.BlockSpec((1, heads, dv, tk), lambda b, i, j: (b, 0, 0, j)),
            x_spec,
            pl.BlockSpec((1, N_MOD, d), lambda b, i, j: (mod_row(b), 0, 0)),
            pl.BlockSpec(lam_p.shape, lambda b, i, j: (0, 0)),
            pl.BlockSpec(subln.shape, lambda b, i, j: (0, 0)),
            _resident(w_out.shape, lambda b, i, j: (0, 0)),
        ],
        out_specs=x_spec,
        out_shape=jax.ShapeDtypeStruct(x.shape, F32),
        input_output_aliases={3: 0},
        scratch_shapes=[
            pltpu.VMEM((2 * heads, 1, tq), F32),
            pltpu.VMEM((2 * heads, 1, tq), F32),
            pltpu.VMEM((2 * heads, dv, tq), F32),
            pltpu.VMEM((tq, d), MXU_DTYPE),
            pltpu.VMEM((tk, tq), F32),
            pltpu.VMEM((tk, tq), F32),
            pltpu.VMEM((1, tq), F32),
            pltpu.VMEM((1, tq), F32),
        ],
        compiler_params=_params(("parallel", "parallel", "arbitrary")),
        name="dif_attn_ctx" if ctx_queries else "dif_attn",
    )(q, k, v, x, mod, lam_p, subln, w_out)


def _diff_tables(rows, tm, dh):
    n_freq = dh // 4
    inv = ROPE_BASE ** (-jnp.arange(n_freq, dtype=F32) / n_freq)
    n_rows = rows.t // GRID_W
    row = jnp.broadcast_to(jnp.arange(n_rows, dtype=F32)[:, None], (n_rows, GRID_W)).reshape(-1)
    col = jnp.broadcast_to(jnp.arange(GRID_W, dtype=F32)[None, :], (n_rows, GRID_W)).reshape(-1)
    ang = jnp.concatenate([row[:, None] * inv, col[:, None] * inv], axis=-1)
    cos, sin = jnp.cos(ang), jnp.sin(ang)
    reps = LANES // dh
    cos_l = jnp.tile(jnp.concatenate([cos, cos], axis=-1), (1, reps))
    sin_l = jnp.tile(jnp.concatenate([-sin, sin], axis=-1), (1, reps))
    ones, zeros = jnp.ones((tm, LANES), F32), jnp.zeros((tm, LANES), F32)
    return jnp.concatenate([ones, cos_l]), jnp.concatenate([zeros, sin_l])


def _mixer_diff(rows, x, mod, g, w_in, lam_p, subln, w_out, lam_init, need_ctx_out):
    d = x.shape[1]
    dh = d // DIFF_HEADS // 2
    tm = rows.tile(ATT_ROW_TILE)
    cos_t, sin_t = _diff_tables(rows, tm, dh)
    q, k, v = _dif_in(rows, x, mod, g, _mx(w_in), cos_t, sin_t, dh)
    args = (mod, lam_p, subln.reshape(1, -1), _mx(w_out), lam_init, dh)
    x = _dif_attn(rows, q, k, v, x, *args, ctx_queries=False)
    if need_ctx_out:
        x = _dif_attn(rows, q, k, v, x, *args, ctx_queries=True)
    return x


def _final_kernel(x_ref, g_ref, o_ref):
    x = x_ref[...]
    o_ref[...] = x * lax.rsqrt(jnp.mean(x * x, axis=-1, keepdims=True) + EPS) * g_ref[...]


def _final_norm(rows, x, g):
    d = x.shape[1]
    tm = rows.tile(1024)
    n_ctx_tiles = rows.b * rows.c // tm
    return pl.pallas_call(
        _final_kernel,
        grid=(rows.b * rows.t // tm,),
        in_specs=[pl.BlockSpec((tm, d), lambda i: (i + n_ctx_tiles, 0)),
                  pl.BlockSpec((1, d), lambda i: (0, 0))],
        out_specs=pl.BlockSpec((tm, d), lambda i: (i, 0)),
        out_shape=jax.ShapeDtypeStruct((rows.b * rows.t, d), F32),
        compiler_params=_params(("parallel",)),
        name="final_norm",
    )(x, g)


def kernel(x, c, ctx, c_ctx, norm_g, mod_w, mod_b, ffn_w_in, ffn_w_out, lru_w_in, lru_conv_w, lru_conv_b,
           lru_gate_w, lru_gate_b, lru_lam, lru_w_out, ret_w_in, ret_w_out, dif_w_in, dif_lam, dif_subln,
           dif_w_out, final_g):
    bsz, n_lat, d = x.shape
    n_ctx = ctx.shape[1]
    depth = norm_g.shape[0]
    assert bsz + 1 <= SUBLANES
    rows = _Rows(bsz, n_ctx, n_lat)

    cond = jnp.zeros((SUBLANES, d), F32).at[:bsz].set(c).at[bsz].set(c_ctx)
    mods = _ada_mod(cond, mod_w, mod_b).reshape(depth, SUBLANES, N_MOD, d)
    xs = jnp.concatenate([ctx.reshape(bsz * n_ctx, d), x.reshape(bsz * n_lat, d)], axis=0)

    fc = 256
    for i in range(depth):
        last = i == depth - 1
        mod = mods[i]
        g = norm_g[i].reshape(3, 1, d)
        xs = _ffn(rows, xs, mod, g[0], *_ffn_weights(ffn_w_in[i, 0], ffn_w_out[i, 0], fc), k0=0)
        kind, j = i % 3, i // 3
        if kind == 0:
            xs = _mixer_rglru(rows, xs, mod, g[1], lru_w_in[j], lru_conv_w[j], lru_conv_b[j], lru_gate_w[j],
                              lru_gate_b[j], lru_lam[j], lru_w_out[j])
        elif kind == 1:
            xs = _mixer_retention(rows, xs, mod, g[1], ret_w_in[j], ret_w_out[j])
        else:
            lam_init = 0.8 - 0.6 * math.exp(-0.3 * i)
            xs = _mixer_diff(rows, xs, mod, g[1], dif_w_in[j], dif_lam[j], dif_subln[j], dif_w_out[j],
                             lam_init, not last)
        xs = _ffn(rows, xs, mod, g[2], *_ffn_weights(ffn_w_in[i, 1], ffn_w_out[i, 1], fc), k0=6)
    return _final_norm(rows, xs, final_g.reshape(1, d)).reshape(bsz, n_lat, d)
```

```python
import functools
import math

import numpy as np
import jax
import jax.numpy as jnp
from jax import lax
from jax.experimental import pallas as pl
from jax.experimental.pallas import tpu as pltpu

F32 = jnp.float32
MXU_DTYPE = jnp.bfloat16

EPS = 1e-6
N_MOD = 9
MACARON_W = 0.5
GRID_W = 64
ROPE_BASE = 10000.0
LRU_BLOCKS = 8
LRU_GROUPS = 2
CONV_W = 4
CONV_PAD_LEFT = 1
LRU_C = 8.0
LRU_CHUNK = 128
LRU_ROW_TILE = 256
RET_HEADS = 4
RET_CHUNK = 128
RET_THETA_BASE = 10000.0
DIFF_HEADS = 8
ATT_TQ = 512
ATT_TK = 1408
ATT_ROW_TILE = 256
ATT_HEAD_UNROLL = 4

LANES = 128
SUBLANES = 8
VMEM_LIMIT = 56 * 1024 * 1024


def _params(sem):
    return pltpu.CompilerParams(dimension_semantics=sem, vmem_limit_bytes=VMEM_LIMIT)


def _resident(shape, index_map):
    return pl.BlockSpec(shape, index_map, pipeline_mode=pl.Buffered(1))


def _mx(v):
    return v.astype(MXU_DTYPE)


def _dot(a, b):
    return jnp.dot(a, b, preferred_element_type=F32)


def _dot_nt(a, b):
    return lax.dot_general(a, b, (((1,), (1,)), ((), ())), preferred_element_type=F32)


def _norm_mod(x, g, shift, scale):
    ms = jnp.mean(x * x, axis=-1, keepdims=True)
    return (x * lax.rsqrt(ms + EPS)) * g * (1.0 + scale) + shift


def _silu(v):
    return v * jax.nn.sigmoid(v)


class _Rows:
    def __init__(self, bsz, n_ctx, n_lat):
        self.b, self.c, self.t = bsz, n_ctx, n_lat
        self.total = bsz * (n_ctx + n_lat)

    def tile(self, cap):
        return math.gcd(math.gcd(self.b * self.c, self.t), cap)

    def mod_row(self, tm):
        n_ctx_tiles = self.b * self.c // tm
        per_batch = self.t // tm
        bsz = self.b
        return lambda i: jnp.where(i < n_ctx_tiles, bsz, (i - n_ctx_tiles) // per_batch)

    def pos_block(self, tm):
        n_ctx_tiles = self.b * self.c // tm
        per_batch = self.t // tm
        return lambda i: jnp.where(i < n_ctx_tiles, 0, 1 + (i - n_ctx_tiles) % per_batch)

    def seq_of_tile(self, tm):
        n_ctx_tiles = self.b * self.c // tm
        ctx_per, lat_per = self.c // tm, self.t // tm

        def f(i):
            r = i - n_ctx_tiles
            is_ctx = i < n_ctx_tiles
            return (jnp.where(is_ctx, i // ctx_per, r // lat_per),
                    jnp.where(is_ctx, i % ctx_per, ctx_per + r % lat_per))
        return f

    def seq_block(self, chunk, reverse):
        ncc, ncl = self.c // chunk, self.t // chunk
        bsz = self.b

        def f(b, s):
            if reverse:
                q_ctx, q_lat = ncc - 1 - s, ncl - 1 - (s - ncc)
            else:
                q_ctx, q_lat = s, s - ncc
            return jnp.where(s < ncc, b * ncc + q_ctx, bsz * ncc + b * ncl + q_lat)
        return f


def _ada_kernel(c_ref, w_ref, b_ref, o_ref):
    s = _mx(_silu(c_ref[...]))
    o_ref[0] = _dot(s, _mx(w_ref[0])) + b_ref[0]


def _ada_mod(cond, mod_w, mod_b):
    depth, d, n = mod_w.shape
    tn = 1024
    return pl.pallas_call(
        _ada_kernel,
        grid=(depth, n // tn),
        in_specs=[
            pl.BlockSpec((SUBLANES, d), lambda l, j: (0, 0)),
            pl.BlockSpec((1, d, tn), lambda l, j: (l, 0, j)),
            pl.BlockSpec((1, 1, tn), lambda l, j: (l, 0, j)),
        ],
        out_specs=pl.BlockSpec((1, SUBLANES, tn), lambda l, j: (l, 0, j)),
        out_shape=jax.ShapeDtypeStruct((depth, SUBLANES, n), F32),
        compiler_params=_params(("parallel", "parallel")),
        name="ada_mod",
    )(cond, mod_w, mod_b.reshape(depth, 1, n))


def _ffn_kernel(*refs, k0, n_chunks, n_ctx_tiles, final):
    n_src = 1 if n_ctx_tiles is None else 2
    mod_ref, g_ref, win_ref, wout_ref = refs[n_src:n_src + 4]
    o_ref = refs[-1]
    if n_ctx_tiles is None:
        x = refs[0][...]
    else:
        x = jnp.where(pl.program_id(0) < n_ctx_tiles, refs[0][...], refs[1][...])
    h = _mx(_norm_mod(x, g_ref[...], mod_ref[0, k0:k0 + 1, :], mod_ref[0, k0 + 1:k0 + 2, :]))
    acc = jnp.zeros(x.shape, F32)
    for c in range(n_chunks):
        a = _dot(h, win_ref[0, c])
        g = _dot(h, win_ref[1, c])
        acc = acc + _dot(_mx(_silu(a) * g), wout_ref[c])
    y = x + (MACARON_W * mod_ref[0, k0 + 2:k0 + 3, :]) * acc
    if final:
        y = y * lax.rsqrt(jnp.mean(y * y, axis=-1, keepdims=True) + EPS) * refs[n_src + 4][...]
    o_ref[...] = y


def _ffn(rows, x, mod, g, w_in, w_out, k0, final_g=None):
    split = isinstance(x, tuple)
    d = x[0].shape[1] if split else x.shape[1]
    n_chunks, fc = w_out.shape[0], w_out.shape[1]
    tm = rows.tile(512)
    n_ctx_tiles = rows.b * rows.c // tm
    first = n_ctx_tiles if final_g is not None else 0
    mod_row = rows.mod_row(tm)
    if split:
        srcs = list(x)
        src_specs = [pl.BlockSpec((tm, d), lambda i: (jnp.minimum(i, n_ctx_tiles - 1), 0)),
                     pl.BlockSpec((tm, d), lambda i: (jnp.maximum(i - n_ctx_tiles, 0), 0))]
    else:
        srcs = [x]
        src_specs = [pl.BlockSpec((tm, d), lambda i: (i + first, 0))]
    extra, extra_specs = ([], []) if final_g is None else ([final_g], [pl.BlockSpec((1, d), lambda i: (0, 0))])
    return pl.pallas_call(
        functools.partial(_ffn_kernel, k0=k0, n_chunks=n_chunks, n_ctx_tiles=n_ctx_tiles if split else None,
                          final=final_g is not None),
        grid=(rows.total // tm - first,),
        in_specs=src_specs + [
            pl.BlockSpec((1, N_MOD, d), lambda i: (mod_row(i + first), 0, 0)),
            pl.BlockSpec((1, d), lambda i: (0, 0)),
            _resident((2, n_chunks, d, fc), lambda i: (0, 0, 0, 0)),
            _resident((n_chunks, fc, d), lambda i: (0, 0, 0)),
        ] + extra_specs,
        out_specs=pl.BlockSpec((tm, d), lambda i: (i, 0)),
        out_shape=jax.ShapeDtypeStruct((rows.total - first * tm, d), F32),
        compiler_params=_params(("parallel",)),
        name="ffn",
    )(*srcs, mod, g, w_in, w_out, *extra)


def _ffn_weights(w_in, w_out, fc):
    d, two_f = w_in.shape
    f = two_f // 2
    w_in = _mx(w_in).reshape(d, 2, f // fc, fc).transpose(1, 2, 0, 3)
    return w_in, _mx(w_out).reshape(f // fc, fc, d)


def _lru_in_kernel(x_ref, mod_ref, g_ref, w_ref, xb_ref, gate_ref, *, d_rnn):
    h = _mx(_norm_mod(x_ref[...], g_ref[...], mod_ref[0, 3:4, :], mod_ref[0, 4:5, :]))
    xb_ref[0] = _dot(h, w_ref[:, :d_rnn])
    gate_ref[...] = jax.nn.gelu(_dot(h, w_ref[:, d_rnn:])).astype(gate_ref.dtype)


def _lru_in(rows, x, mod, g, w_in):
    d = x.shape[1]
    d_rnn = w_in.shape[1] // 2
    tm = rows.tile(LRU_ROW_TILE)
    seq = rows.seq_of_tile(tm)
    return pl.pallas_call(
        functools.partial(_lru_in_kernel, d_rnn=d_rnn),
        grid=(rows.total // tm,),
        in_specs=[
            pl.BlockSpec((tm, d), lambda i: (i, 0)),
            pl.BlockSpec((1, N_MOD, d), lambda i, f=rows.mod_row(tm): (f(i), 0, 0)),
            pl.BlockSpec((1, d), lambda i: (0, 0)),
            _resident((d, 2 * d_rnn), lambda i: (0, 0)),
        ],
        out_specs=[pl.BlockSpec((1, tm, d_rnn), lambda i: seq(i) + (0,)),
                   pl.BlockSpec((tm, d_rnn), lambda i: (i, 0))],
        out_shape=[jax.ShapeDtypeStruct((rows.b, rows.c + rows.t, d_rnn), F32),
                   jax.ShapeDtypeStruct((rows.total, d_rnn), MXU_DTYPE)],
        compiler_params=_params(("parallel",)),
        name="lru_in",
    )(x, mod, g, w_in)


def _lru_scan_kernel(main_ref, prev_ref, next_ref, cw_ref, cb_ref, gw_ref, gb_ref, lam_ref, out_ref,
                     ext_ref, a_ref, b_ref, h_ref, *, ncc, ncl, pitch):
    bsz, tc, d_rnn = main_ref.shape
    d = pl.program_id(0)
    s = pl.program_id(1)
    n_slabs = d_rnn // LANES
    gw = d_rnn // LRU_GROUPS

    @pl.when(s == 0)
    def _():
        h_ref[...] = jnp.zeros(h_ref.shape, F32)

    q_rev = jnp.where(s < ncc, ncc - 1 - s, ncl - 1 - (s - ncc))
    q_fwd = jnp.where(s < ncc, s, s - ncc)
    q = jnp.where(d == 0, q_fwd, q_rev)
    n_seg = jnp.where(s < ncc, ncc, ncl)
    has_prev = (q > 0).astype(F32)
    has_next = (q < n_seg - 1).astype(F32)

    nl = -lam_ref[0]
    sp = jnp.maximum(nl, 0.0) + jnp.log1p(jnp.exp(-jnp.abs(nl)))
    half_rate = (-0.5 * LRU_C * math.log2(math.e)) * sp

    half = tc // 2
    base = SUBLANES - CONV_PAD_LEFT
    def conv(b, slab):
        ls = slice(slab * LANES, (slab + 1) * LANES)
        ext_ref[b, slab, 0:SUBLANES, :] = prev_ref[b, :, ls] * has_prev
        ext_ref[b, slab, SUBLANES:SUBLANES + tc, :] = main_ref[b, :, ls]
        ext_ref[b, slab, SUBLANES + tc:, :] = next_ref[b, :, ls] * has_next
        parities = []
        for par in range(2):
            acc = cb_ref[:, ls] + cw_ref[0:1, ls] * ext_ref[b, slab, pl.ds(base + par, half, stride=2), :]
            for k in range(1, CONV_W):
                acc = acc + cw_ref[k:k + 1, ls] * ext_ref[b, slab, pl.ds(base + par + k, half, stride=2), :]
            parities.append(acc)
        return jnp.concatenate(parities, axis=0)

    per_group = gw // LANES
    for b in range(bsz):
        for gi in range(LRU_GROUPS):
            xg = jnp.concatenate([conv(b, gi * per_group + j) for j in range(per_group)], axis=1)
            t = jnp.tanh(_dot(_mx(xg), gw_ref[0, gi]) + gb_ref[0, gi])
            rate = half_rate[:, gi * gw:(gi + 1) * gw]
            a = jnp.exp2(rate * t[:, :gw] + rate)
            root = jnp.exp2(0.5 * jnp.log2(1.0 - a * a))
            bb = root * ((0.5 * xg) * (t[:, gw:] + 1.0))
            for j in range(per_group):
                ls = slice(j * LANES, (j + 1) * LANES)
                for par in range(2):
                    rows_p = slice(par * half, (par + 1) * half)
                    dst = (gi * per_group + j, pl.ds(b * pitch + par, half, stride=2), slice(None))
                    a_ref[dst] = a[rows_p, ls]
                    b_ref[dst] = bb[rows_p, ls]

    def step(t, hs):
        row = t + d * (tc - 1 - 2 * t)
        new = []
        for slab in range(n_slabs):
            idx = (slab, pl.ds(row, bsz, stride=pitch), slice(None))
            h = a_ref[idx] * hs[slab] + b_ref[idx]
            b_ref[idx] = h
            new.append(h)
        return tuple(new)

    hs = lax.fori_loop(0, tc, step, tuple(h_ref[slab] for slab in range(n_slabs)), unroll=8)
    for slab in range(n_slabs):
        h_ref[slab] = hs[slab]

    for b in range(bsz):
        for slab in range(n_slabs):
            out_ref[0, b, :, slab * LANES:(slab + 1) * LANES] = (
                b_ref[slab, b * pitch:b * pitch + tc, :].astype(out_ref.dtype))


def _lru_scan(rows, xb, conv_w, conv_b, gate_w, gate_b, lam):
    bsz, n_seq, d_rnn = xb.shape
    tc = LRU_CHUNK
    ncc, ncl = rows.c // tc, rows.t // tc
    gw = d_rnn // LRU_GROUPS
    pitch = tc + SUBLANES
    per8 = tc // SUBLANES
    last8 = n_seq // SUBLANES - 1

    def chunk(d, s):
        rev = jnp.where(s < ncc, ncc - 1 - s, ncc + ncl - 1 - (s - ncc))
        return jnp.where(d == 0, s, rev)

    n_slabs = d_rnn // LANES
    return pl.pallas_call(
        functools.partial(_lru_scan_kernel, ncc=ncc, ncl=ncl, pitch=pitch),
        grid=(2, ncc + ncl),
        in_specs=[
            pl.BlockSpec((bsz, tc, d_rnn), lambda d, s: (0, chunk(d, s), 0)),
            pl.BlockSpec((bsz, SUBLANES, d_rnn), lambda d, s: (0, jnp.maximum(chunk(d, s) * per8 - 1, 0), 0)),
            pl.BlockSpec((bsz, SUBLANES, d_rnn),
                         lambda d, s: (0, jnp.minimum((chunk(d, s) + 1) * per8, last8), 0)),
            pl.BlockSpec((CONV_W, d_rnn), lambda d, s: (0, 0)),
            pl.BlockSpec((1, d_rnn), lambda d, s: (0, 0)),
            pl.BlockSpec((1, LRU_GROUPS, gw, 2 * gw), lambda d, s: (d, 0, 0, 0)),
            pl.BlockSpec((1, LRU_GROUPS, 1, 2 * gw), lambda d, s: (d, 0, 0, 0)),
            pl.BlockSpec((1, 1, d_rnn), lambda d, s: (d, 0, 0)),
        ],
        out_specs=pl.BlockSpec((1, bsz, tc, d_rnn), lambda d, s: (d, 0, chunk(d, s), 0)),
        out_shape=jax.ShapeDtypeStruct((2, bsz, n_seq, d_rnn), MXU_DTYPE),
        scratch_shapes=[
            pltpu.VMEM((bsz, n_slabs, tc + 2 * SUBLANES, LANES), F32),
            pltpu.VMEM((n_slabs, bsz * pitch, LANES), F32),
            pltpu.VMEM((n_slabs, bsz * pitch, LANES), F32),
            pltpu.VMEM((n_slabs, bsz, LANES), F32),
        ],
        compiler_params=_params(("arbitrary", "arbitrary")),
        name="lru_scan",
    )(xb, xb, xb, conv_w, conv_b, gate_w, gate_b, lam)


def _lru_gate_weights(gate_w, gate_b):
    n_dir, n_gate, g, blk, _ = gate_w.shape
    per = g // LRU_GROUPS
    gw = per * blk
    w = gate_w.reshape(n_dir, n_gate, LRU_GROUPS, per, blk, blk)
    eye = jnp.eye(per, dtype=gate_w.dtype)
    dense = jnp.einsum('dkgpij,pq->dgpikqj', w, eye).reshape(n_dir, LRU_GROUPS, gw, n_gate * gw)
    bias = gate_b.reshape(n_dir, n_gate, LRU_GROUPS, gw).transpose(0, 2, 1, 3)
    return _mx(0.5 * dense), 0.5 * bias.reshape(n_dir, LRU_GROUPS, 1, n_gate * gw)


def _lru_out_kernel(hf_ref, hb_ref, gate_ref, x_ref, mod_ref, w_ref, o_ref):
    hsum = hf_ref[0, 0].astype(F32) + hb_ref[0, 0].astype(F32)
    y = _dot(_mx(hsum * gate_ref[...].astype(F32)), w_ref[...])
    o_ref[...] = x_ref[...] + mod_ref[0, 5:6, :] * y


def _lru_out(rows, hs, gate, x, mod, w_out):
    d = x.shape[1]
    d_rnn = gate.shape[1]
    tm = rows.tile(LRU_ROW_TILE)
    seq = rows.seq_of_tile(tm)
    return pl.pallas_call(
        _lru_out_kernel,
        grid=(rows.total // tm,),
        in_specs=[
            pl.BlockSpec((1, 1, tm, d_rnn), lambda i: (0,) + seq(i) + (0,)),
            pl.BlockSpec((1, 1, tm, d_rnn), lambda i: (1,) + seq(i) + (0,)),
            pl.BlockSpec((tm, d_rnn), lambda i: (i, 0)),
            pl.BlockSpec((tm, d), lambda i: (i, 0)),
            pl.BlockSpec((1, N_MOD, d), lambda i, f=rows.mod_row(tm): (f(i), 0, 0)),
            _resident((d_rnn, d), lambda i: (0, 0)),
        ],
        out_specs=pl.BlockSpec((tm, d), lambda i: (i, 0)),
        out_shape=jax.ShapeDtypeStruct(x.shape, F32),
        compiler_params=_params(("parallel",)),
        name="lru_out",
    )(hs, hs, gate, x, mod, w_out)


def _mixer_rglru(rows, x, mod, g, w_in, conv_w, conv_b, gate_w, gate_b, lam, w_out):
    xb, gate = _lru_in(rows, x, mod, g, _mx(w_in))
    gw_dense, gb = _lru_gate_weights(gate_w, gate_b)
    hs = _lru_scan(rows, xb, conv_w, conv_b.reshape(1, -1), gw_dense, gb,
                   lam.reshape(2, 1, -1))
    return _lru_out(rows, hs, gate, x, mod, _mx(w_out))


def _ret_in_kernel(x_ref, mod_ref, g_ref, w_ref, cos_ref, sin_ref, q_ref, k_ref, v_ref, sg_ref,
                   *, hdk, hdv, dk):
    h = _mx(_norm_mod(x_ref[...], g_ref[...], mod_ref[0, 3:4, :], mod_ref[0, 4:5, :]))
    cos, sin = cos_ref[...], sin_ref[...]
    half = dk // 2

    def rope(z):
        parts = []
        for hd in range(hdk // dk):
            z1 = z[:, hd * dk:hd * dk + half]
            z2 = z[:, hd * dk + half:(hd + 1) * dk]
            parts += [z1 * cos - z2 * sin, z1 * sin + z2 * cos]
        return jnp.concatenate(parts, axis=-1)

    q_ref[...] = rope(_dot(h, w_ref[:, :hdk])).astype(q_ref.dtype)
    k_ref[...] = (rope(_dot(h, w_ref[:, hdk:2 * hdk])) * (dk ** -0.5)).astype(k_ref.dtype)
    v_ref[...] = _dot(h, w_ref[:, 2 * hdk:2 * hdk + hdv]).astype(v_ref.dtype)
    sg_ref[...] = _silu(_dot(h, w_ref[:, 2 * hdk + hdv:])).astype(sg_ref.dtype)


def _ret_in(rows, x, mod, g, w_in, cos_t, sin_t):
    d = x.shape[1]
    hdk = d
    hdv = (w_in.shape[1] - 2 * hdk) // 2
    dk = hdk // RET_HEADS
    tm = rows.tile(512)
    pos = rows.pos_block(tm)
    row_spec = lambda n: pl.BlockSpec((tm, n), lambda i: (i, 0))
    return pl.pallas_call(
        functools.partial(_ret_in_kernel, hdk=hdk, hdv=hdv, dk=dk),
        grid=(rows.total // tm,),
        in_specs=[
            row_spec(d),
            pl.BlockSpec((1, N_MOD, d), lambda i, f=rows.mod_row(tm): (f(i), 0, 0)),
            pl.BlockSpec((1, d), lambda i: (0, 0)),
            _resident(w_in.shape, lambda i: (0, 0)),
            pl.BlockSpec((tm, dk // 2), lambda i: (pos(i), 0)),
            pl.BlockSpec((tm, dk // 2), lambda i: (pos(i), 0)),
        ],
        out_specs=[row_spec(hdk), row_spec(hdk), row_spec(hdv), row_spec(hdv)],
        out_shape=[jax.ShapeDtypeStruct((rows.total, n), MXU_DTYPE) for n in (hdk, hdk, hdv, hdv)],
        compiler_params=_params(("parallel",)),
        name="ret_in",
    )(x, mod, g, w_in, cos_t, sin_t)


def _ret_log_gamma(hd):
    return float(np.log1p(-np.exp2(np.float32(-5.0 - hd)), dtype=np.float32))


def _ret_state_step(q, k, v, s_ref, hd, lg, q_pow, k_pow):
    s_old = s_ref[hd]
    o = _dot(_mx(q.astype(F32) * jnp.exp(lg * q_pow)), _mx(s_old))
    kd = (k.astype(F32) * jnp.exp(lg * k_pow)).T
    s_ref[hd] = s_old * math.exp(lg * q.shape[0]) + _dot(_mx(kd), v)
    return o


def _ret_fwd_kernel(q_ref, k_ref, v_ref, o_ref, s_ref, *, dk, dv):
    tc = q_ref.shape[0]

    @pl.when(pl.program_id(1) == 0)
    def _():
        s_ref[...] = jnp.zeros(s_ref.shape, F32)

    row = lax.broadcasted_iota(jnp.int32, (tc, tc), 0)
    col = lax.broadcasted_iota(jnp.int32, (tc, tc), 1)
    dist = jnp.abs(row - col).astype(F32)
    idx = lax.broadcasted_iota(jnp.int32, (tc, 1), 0).astype(F32)
    for hd in range(RET_HEADS):
        lg = _ret_log_gamma(hd)
        q = q_ref[:, hd * dk:(hd + 1) * dk]
        k = k_ref[:, hd * dk:(hd + 1) * dk]
        v = v_ref[:, hd * dv:(hd + 1) * dv]
        scores = _dot_nt(q, k) * jnp.exp(lg * dist)
        o = _dot(_mx(scores), v)
        o = o + _ret_state_step(q, k, v, s_ref, hd, lg, idx + 1.0, (tc - 1.0) - idx)
        o_ref[:, hd * dv:(hd + 1) * dv] = o.astype(o_ref.dtype)


def _ret_bwd_kernel(q_ref, k_ref, v_ref, of_ref, sg_ref, x_ref, mod_ref, w_ref, o_ref, s_ref, y_ref,
                    *, dk, dv):
    tc = q_ref.shape[0]

    @pl.when(pl.program_id(1) == 0)
    def _():
        s_ref[...] = jnp.zeros(s_ref.shape, F32)

    idx = lax.broadcasted_iota(jnp.int32, (tc, 1), 0).astype(F32)
    for hd in range(RET_HEADS):
        lg = _ret_log_gamma(hd)
        q = q_ref[:, hd * dk:(hd + 1) * dk]
        k = k_ref[:, hd * dk:(hd + 1) * dk]
        v = v_ref[:, hd * dv:(hd + 1) * dv]
        o = of_ref[:, hd * dv:(hd + 1) * dv].astype(F32) + _ret_state_step(q, k, v, s_ref, hd, lg, tc - idx, idx)
        on = o * lax.rsqrt(jnp.mean(o * o, axis=-1, keepdims=True) + EPS)
        y_ref[:, hd * dv:(hd + 1) * dv] = _mx(on * sg_ref[:, hd * dv:(hd + 1) * dv].astype(F32))
    o_ref[...] = x_ref[...] + mod_ref[0, 5:6, :] * _dot(y_ref[...], w_ref[...])


def _ret_core(rows, q, k, v, sg, x, mod, w_out):
    d = x.shape[1]
    hdk, hdv = q.shape[1], v.shape[1]
    dk, dv = hdk // RET_HEADS, hdv // RET_HEADS
    tc = RET_CHUNK
    n_steps = (rows.c + rows.t) // tc
    ncc = rows.c // tc
    bsz = rows.b
    fwd, bwd = rows.seq_block(tc, False), rows.seq_block(tc, True)
    seq = lambda f, n: pl.BlockSpec((tc, n), lambda b, s: (f(b, s), 0))
    o_part = pl.pallas_call(
        functools.partial(_ret_fwd_kernel, dk=dk, dv=dv),
        grid=(bsz, n_steps),
        in_specs=[seq(fwd, hdk), seq(fwd, hdk), seq(fwd, hdv)],
        out_specs=seq(fwd, hdv),
        out_shape=jax.ShapeDtypeStruct((rows.total, hdv), MXU_DTYPE),
        scratch_shapes=[pltpu.VMEM((RET_HEADS, dk, dv), F32)],
        compiler_params=_params(("parallel", "arbitrary")),
        name="ret_fwd",
    )(q, k, v)
    return pl.pallas_call(
        functools.partial(_ret_bwd_kernel, dk=dk, dv=dv),
        grid=(bsz, n_steps),
        in_specs=[seq(bwd, hdk), seq(bwd, hdk), seq(bwd, hdv), seq(bwd, hdv), seq(bwd, hdv), seq(bwd, d),
                  pl.BlockSpec((1, N_MOD, d), lambda b, s: (jnp.where(s < ncc, bsz, b), 0, 0)),
                  _resident((hdv, d), lambda b, s: (0, 0))],
        out_specs=seq(bwd, d),
        out_shape=jax.ShapeDtypeStruct(x.shape, F32),
        scratch_shapes=[pltpu.VMEM((RET_HEADS, dk, dv), F32), pltpu.VMEM((tc, hdv), MXU_DTYPE)],
        compiler_params=_params(("parallel", "arbitrary")),
        name="ret_bwd",
    )(q, k, v, o_part, sg, x, mod, w_out)


def _retention_tables(rows, tm, dk):
    theta = RET_THETA_BASE ** (-jnp.linspace(0.0, 1.0, dk // 2, dtype=F32))
    ang = jnp.arange(rows.t, dtype=F32)[:, None] * theta
    ones, zeros = jnp.ones((tm, dk // 2), F32), jnp.zeros((tm, dk // 2), F32)
    return jnp.concatenate([ones, jnp.cos(ang)]), jnp.concatenate([zeros, jnp.sin(ang)])


def _mixer_retention(rows, x, mod, g, w_in, w_out):
    tm = rows.tile(512)
    cos_t, sin_t = _retention_tables(rows, tm, x.shape[1] // RET_HEADS)
    q, k, v, sg = _ret_in(rows, x, mod, g, _mx(w_in), cos_t, sin_t)
    return _ret_core(rows, q, k, v, sg, x, mod, _mx(w_out))


def _dif_in_kernel(x_ref, mod_ref, g_ref, w_ref, cos_ref, sin_ref, q_ref, k_ref, vt_ref, *, dh):
    d = x_ref.shape[1]
    h = _mx(_norm_mod(x_ref[...], g_ref[...], mod_ref[0, 3:4, :], mod_ref[0, 4:5, :]))
    cos, sin = cos_ref[...], sin_ref[...]
    lane = lax.broadcasted_iota(jnp.int32, cos.shape, 1)
    first_half = (lane % dh) < (dh // 2)

    def rope(zj):
        swapped = jnp.where(first_half, pltpu.roll(zj, LANES - dh // 2, 1), pltpu.roll(zj, dh // 2, 1))
        return zj * cos + swapped * sin

    q = _dot(h, w_ref[:, :d])
    k = _dot(h, w_ref[:, d:2 * d])
    v = _dot(h, w_ref[:, 2 * d:])
    first_map = lane < dh
    for hd in range(d // LANES):
        qh = rope(q[:, hd * LANES:(hd + 1) * LANES]) * (dh ** -0.5 * math.log2(math.e))
        q_ref[2 * hd] = jnp.where(first_map, qh, 0.0).astype(q_ref.dtype)
        q_ref[2 * hd + 1] = jnp.where(first_map, 0.0, qh).astype(q_ref.dtype)
        k_ref[0, hd] = rope(k[:, hd * LANES:(hd + 1) * LANES]).astype(k_ref.dtype)
        vt_ref[0, hd] = v[:, hd * LANES:(hd + 1) * LANES].T.astype(vt_ref.dtype)


def _dif_in(rows, x, mod, g, w_in, cos_t, sin_t, dh):
    d = x.shape[1]
    heads = d // LANES
    n_seq = rows.c + rows.t
    tm = rows.tile(ATT_ROW_TILE)
    pos = rows.pos_block(tm)
    seq = rows.seq_of_tile(tm)
    return pl.pallas_call(
        functools.partial(_dif_in_kernel, dh=dh),
        grid=(rows.total // tm,),
        in_specs=[
            pl.BlockSpec((tm, d), lambda i: (i, 0)),
            pl.BlockSpec((1, N_MOD, d), lambda i, f=rows.mod_row(tm): (f(i), 0, 0)),
            pl.BlockSpec((1, d), lambda i: (0, 0)),
            _resident(w_in.shape, lambda i: (0, 0)),
            pl.BlockSpec((tm, LANES), lambda i: (pos(i), 0)),
            pl.BlockSpec((tm, LANES), lambda i: (pos(i), 0)),
        ],
        out_specs=[
            pl.BlockSpec((2 * heads, tm, LANES), lambda i: (0, i, 0)),
            pl.BlockSpec((1, heads, tm, LANES), lambda i: (seq(i)[0], 0, seq(i)[1], 0)),
            pl.BlockSpec((1, heads, LANES, tm), lambda i: (seq(i)[0], 0, 0, seq(i)[1])),
        ],
        out_shape=[
            jax.ShapeDtypeStruct((2 * heads, rows.total, LANES), MXU_DTYPE),
            jax.ShapeDtypeStruct((rows.b, heads, n_seq, LANES), MXU_DTYPE),
            jax.ShapeDtypeStruct((rows.b, heads, LANES, n_seq), MXU_DTYPE),
        ],
        compiler_params=_params(("parallel",)),
        name="dif_in",
    )(x, mod, g, w_in, cos_t, sin_t)


def _dif_attn_kernel(q_ref, k_ref, vt_ref, x_ref, mod_ref, lam_ref, sub_ref, w_ref, o_ref,
                     m_ref, l_ref, acc_ref, y_ref, s0_ref, s1_ref, mc0_ref, mc1_ref, *, dh, lam_init):
    j = pl.program_id(2)
    dv = 2 * dh
    n_heads = k_ref.shape[1]

    @pl.when(j == 0)
    def _():
        m_ref[...] = jnp.full(m_ref.shape, -jnp.inf, F32)
        l_ref[...] = jnp.zeros(l_ref.shape, F32)
        acc_ref[...] = jnp.zeros(acc_ref.shape, F32)

    def scores(hd, col, s_ref, mc_ref):
        st = _dot_nt(k_ref[0, hd], q_ref[col])
        s_ref[...] = st
        mc_ref[...] = jnp.max(st, axis=0, keepdims=True)

    def accumulate(hd, col, s_ref, mc_ref):
        m_old = m_ref[col]
        m_new = jnp.maximum(m_old, mc_ref[...])
        alpha = jnp.exp2(m_old - m_new)
        pt = jnp.exp2(s_ref[...] - m_new)
        l_ref[col] = alpha * l_ref[col] + jnp.sum(pt, axis=0, keepdims=True)
        acc_ref[col] = alpha * acc_ref[col] + _dot(vt_ref[0, hd], _mx(pt))
        m_ref[col] = m_new

    scores(0, 0, s0_ref, mc0_ref)

    def head(hd, carry):
        scores(hd, 2 * hd + 1, s1_ref, mc1_ref)
        accumulate(hd, 2 * hd, s0_ref, mc0_ref)
        nxt = jnp.minimum(hd + 1, n_heads - 1)
        scores(nxt, 2 * nxt, s0_ref, mc0_ref)
        accumulate(hd, 2 * hd + 1, s1_ref, mc1_ref)
        return carry

    lax.fori_loop(0, n_heads, head, 0, unroll=ATT_HEAD_UNROLL)

    @pl.when(j == pl.num_programs(2) - 1)
    def _():
        lp = lam_ref[...]
        lam = (jnp.exp(jnp.sum(lp[0:1] * lp[1:2], axis=-1, keepdims=True))
               - jnp.exp(jnp.sum(lp[2:3] * lp[3:4], axis=-1, keepdims=True)) + lam_init)
        for hd in range(n_heads):
            ot = acc_ref[2 * hd] / l_ref[2 * hd] - lam * (acc_ref[2 * hd + 1] / l_ref[2 * hd + 1])
            o = ot.T
            on = o * lax.rsqrt(jnp.mean(o * o, axis=-1, keepdims=True) + EPS) * sub_ref[...]
            y_ref[:, hd * dv:(hd + 1) * dv] = _mx(on * (1.0 - lam_init))
        o_ref[...] = x_ref[...] + mod_ref[0, 5:6, :] * _dot(y_ref[...], w_ref[...])


def _largest_tile(n, cap):
    return max(t for t in range(LANES, min(n, cap) + 1, LANES) if n % t == 0)


def _dif_attn(rows, q, k, vt, x, mod, lam_p, subln, w_out, lam_init, dh, ctx_queries):
    d = x.shape[1]
    bsz = rows.b
    if ctx_queries:
        tq = math.gcd(ATT_TQ, rows.c)
        tk = _largest_tile(rows.c, ATT_TK)
        nq, n_kv, q0, mod_row = rows.c // tq, rows.c // tk, 0, (lambda b: bsz)
    else:
        tq = math.gcd(ATT_TQ, rows.t)
        tk = _largest_tile(rows.c + rows.t, ATT_TK)
        nq, n_kv, q0, mod_row = rows.t // tq, (rows.c + rows.t) // tk, bsz * rows.c // tq, (lambda b: b)
    heads, dv = k.shape[1], k.shape[3]
    x_spec = pl.BlockSpec((tq, d), lambda b, i, j: (q0 + b * nq + i, 0))
    return pl.pallas_call(
        functools.partial(_dif_attn_kernel, dh=dh, lam_init=lam_init),
        grid=(bsz, nq, n_kv),
        in_specs=[
            pl.BlockSpec((2 * heads, tq, dv), lambda b, i, j: (0, q0 + b * nq + i, 0)),
            pl.BlockSpec((1, heads, tk, dv), lambda b, i, j: (b, 0, j, 0)),
            pl.BlockSpec((1, heads, dv, tk), lambda b, i, j: (b, 0, 0, j)),
            x_spec,
            pl.BlockSpec((1, N_MOD, d), lambda b, i, j: (mod_row(b), 0, 0)),
            pl.BlockSpec(lam_p.shape, lambda b, i, j: (0, 0)),
            pl.BlockSpec(subln.shape, lambda b, i, j: (0, 0)),
            _resident(w_out.shape, lambda b, i, j: (0, 0)),
        ],
        out_specs=x_spec,
        out_shape=jax.ShapeDtypeStruct(x.shape, F32),
        input_output_aliases={3: 0},
        scratch_shapes=[
            pltpu.VMEM((2 * heads, 1, tq), F32),
            pltpu.VMEM((2 * heads, 1, tq), F32),
            pltpu.VMEM((2 * heads, dv, tq), F32),
            pltpu.VMEM((tq, d), MXU_DTYPE),
            pltpu.VMEM((tk, tq), F32),
            pltpu.VMEM((tk, tq), F32),
            pltpu.VMEM((1, tq), F32),
            pltpu.VMEM((1, tq), F32),
        ],
        compiler_params=_params(("parallel", "parallel", "arbitrary")),
        name="dif_attn_ctx" if ctx_queries else "dif_attn",
    )(q, k, vt, x, mod, lam_p, subln, w_out)


def _diff_tables(rows, tm, dh):
    n_freq = dh // 4
    inv = ROPE_BASE ** (-jnp.arange(n_freq, dtype=F32) / n_freq)
    n_rows = rows.t // GRID_W
    row = jnp.broadcast_to(jnp.arange(n_rows, dtype=F32)[:, None], (n_rows, GRID_W)).reshape(-1)
    col = jnp.broadcast_to(jnp.arange(GRID_W, dtype=F32)[None, :], (n_rows, GRID_W)).reshape(-1)
    ang = jnp.concatenate([row[:, None] * inv, col[:, None] * inv], axis=-1)
    cos, sin = jnp.cos(ang), jnp.sin(ang)
    reps = LANES // dh
    cos_l = jnp.tile(jnp.concatenate([cos, cos], axis=-1), (1, reps))
    sin_l = jnp.tile(jnp.concatenate([-sin, sin], axis=-1), (1, reps))
    ones, zeros = jnp.ones((tm, LANES), F32), jnp.zeros((tm, LANES), F32)
    return jnp.concatenate([ones, cos_l]), jnp.concatenate([zeros, sin_l])


def _mixer_diff(rows, x, mod, g, w_in, lam_p, subln, w_out, lam_init, need_ctx_out):
    d = x.shape[1]
    dh = d // DIFF_HEADS // 2
    cos_t, sin_t = _diff_tables(rows, rows.tile(ATT_ROW_TILE), dh)
    q, k, vt = _dif_in(rows, x, mod, g, _mx(w_in), cos_t, sin_t, dh)
    args = (mod, lam_p, subln.reshape(1, -1), _mx(w_out), lam_init, dh)
    x = _dif_attn(rows, q, k, vt, x, *args, ctx_queries=False)
    if need_ctx_out:
        x = _dif_attn(rows, q, k, vt, x, *args, ctx_queries=True)
    return x


def kernel(x, c, ctx, c_ctx, norm_g, mod_w, mod_b, ffn_w_in, ffn_w_out, lru_w_in, lru_conv_w, lru_conv_b,
           lru_gate_w, lru_gate_b, lru_lam, lru_w_out, ret_w_in, ret_w_out, dif_w_in, dif_lam, dif_subln,
           dif_w_out, final_g):
    bsz, n_lat, d = x.shape
    n_ctx = ctx.shape[1]
    depth = norm_g.shape[0]
    assert bsz + 1 <= SUBLANES
    rows = _Rows(bsz, n_ctx, n_lat)

    cond = jnp.zeros((SUBLANES, d), F32).at[:bsz].set(c).at[bsz].set(c_ctx)
    mods = _ada_mod(cond, mod_w, mod_b).reshape(depth, SUBLANES, N_MOD, d)
    xs = (ctx.reshape(bsz * n_ctx, d), x.reshape(bsz * n_lat, d))

    fc = 256
    for i in range(depth):
        last = i == depth - 1
        mod = mods[i]
        g = norm_g[i].reshape(3, 1, d)
        xs = _ffn(rows, xs, mod, g[0], *_ffn_weights(ffn_w_in[i, 0], ffn_w_out[i, 0], fc), k0=0)
        kind, j = i % 3, i // 3
        if kind == 0:
            xs = _mixer_rglru(rows, xs, mod, g[1], lru_w_in[j], lru_conv_w[j], lru_conv_b[j], lru_gate_w[j],
                              lru_gate_b[j], lru_lam[j], lru_w_out[j])
        elif kind == 1:
            xs = _mixer_retention(rows, xs, mod, g[1], ret_w_in[j], ret_w_out[j])
        else:
            lam_init = 0.8 - 0.6 * math.exp(-0.3 * i)
            xs = _mixer_diff(rows, xs, mod, g[1], dif_w_in[j], dif_lam[j], dif_subln[j], dif_w_out[j],
                             lam_init, not last)
        xs = _ffn(rows, xs, mod, g[2], *_ffn_weights(ffn_w_in[i, 1], ffn_w_out[i, 1], fc), k0=6,
                  final_g=final_g.reshape(1, d) if last else None)
    return xs.reshape(bsz, n_lat, d)
```

```python
import functools
import math

import numpy as np
import jax
import jax.numpy as jnp
from jax import lax
from jax.experimental import pallas as pl
from jax.experimental.pallas import tpu as pltpu

F32 = jnp.float32
MXU_DTYPE = jnp.bfloat16

EPS = 1e-6
N_MOD = 9
ADA_TILE = 2304
MACARON_W = 0.5
FFN_ROW_TILE = 1024
GRID_W = 64
ROPE_BASE = 10000.0
LRU_BLOCKS = 8
LRU_GROUPS = 2
CONV_W = 4
CONV_PAD_LEFT = 1
LRU_C = 8.0
LRU_CHUNK = 128
LRU_ROW_TILE = 256
RET_HEADS = 4
RET_CHUNK = 128
RET_BLOCK = 256
RET_THETA_BASE = 10000.0
DIFF_HEADS = 8
ATT_TQ = 512
ATT_TK = 1408
ATT_ROW_TILE = 256
ATT_HEAD_UNROLL = 4

LANES = 128
SUBLANES = 8
VMEM_LIMIT = 56 * 1024 * 1024


def _params(sem):
    return pltpu.CompilerParams(dimension_semantics=sem, vmem_limit_bytes=VMEM_LIMIT)


def _resident(shape, index_map):
    return pl.BlockSpec(shape, index_map, pipeline_mode=pl.Buffered(1))


def _mx(v):
    return v.astype(MXU_DTYPE)


def _dot(a, b):
    return jnp.dot(a, b, preferred_element_type=F32)


def _dot_nt(a, b):
    return lax.dot_general(a, b, (((1,), (1,)), ((), ())), preferred_element_type=F32)


def _norm_mod(x, g, shift, scale):
    ms = jnp.mean(x * x, axis=-1, keepdims=True)
    return (x * lax.rsqrt(ms + EPS)) * g * (1.0 + scale) + shift


def _silu(v):
    return v * jax.nn.sigmoid(v)


class _Rows:
    def __init__(self, bsz, n_ctx, n_lat):
        self.b, self.c, self.t = bsz, n_ctx, n_lat
        self.total = bsz * (n_ctx + n_lat)

    def tile(self, cap):
        return math.gcd(math.gcd(self.b * self.c, self.t), cap)

    def mod_row(self, tm):
        n_ctx_tiles = self.b * self.c // tm
        per_batch = self.t // tm
        bsz = self.b
        return lambda i: jnp.where(i < n_ctx_tiles, bsz, (i - n_ctx_tiles) // per_batch)

    def pos_block(self, tm):
        n_ctx_tiles = self.b * self.c // tm
        per_batch = self.t // tm
        return lambda i: jnp.where(i < n_ctx_tiles, 0, 1 + (i - n_ctx_tiles) % per_batch)

    def seq_of_tile(self, tm):
        n_ctx_tiles = self.b * self.c // tm
        ctx_per, lat_per = self.c // tm, self.t // tm

        def f(i):
            r = i - n_ctx_tiles
            is_ctx = i < n_ctx_tiles
            return (jnp.where(is_ctx, i // ctx_per, r // lat_per),
                    jnp.where(is_ctx, i % ctx_per, ctx_per + r % lat_per))
        return f

    def seq_block(self, chunk, reverse):
        ncc, ncl = self.c // chunk, self.t // chunk
        bsz = self.b

        def f(b, s):
            if reverse:
                q_ctx, q_lat = ncc - 1 - s, ncl - 1 - (s - ncc)
            else:
                q_ctx, q_lat = s, s - ncc
            return jnp.where(s < ncc, b * ncc + q_ctx, bsz * ncc + b * ncl + q_lat)
        return f


def _ada_kernel(c_ref, w_ref, b_ref, o_ref):
    s = _mx(_silu(c_ref[...]))
    o_ref[0] = _dot(s, _mx(w_ref[0])) + b_ref[0]


def _ada_mod(cond, mod_w, mod_b):
    depth, d, n = mod_w.shape
    tn = _largest_tile(n, ADA_TILE)
    return pl.pallas_call(
        _ada_kernel,
        grid=(depth, n // tn),
        in_specs=[
            pl.BlockSpec((SUBLANES, d), lambda l, j: (0, 0)),
            pl.BlockSpec((1, d, tn), lambda l, j: (l, 0, j)),
            pl.BlockSpec((1, 1, tn), lambda l, j: (l, 0, j)),
        ],
        out_specs=pl.BlockSpec((1, SUBLANES, tn), lambda l, j: (l, 0, j)),
        out_shape=jax.ShapeDtypeStruct((depth, SUBLANES, n), F32),
        compiler_params=_params(("parallel", "parallel")),
        name="ada_mod",
    )(cond, mod_w, mod_b.reshape(depth, 1, n))


def _ffn_kernel(*refs, k0, n_chunks, n_ctx_tiles, final):
    n_src = 1 if n_ctx_tiles is None else 2
    mod_ref, g_ref, win_ref, wout_ref = refs[n_src:n_src + 4]
    o_ref = refs[-1]
    if n_ctx_tiles is None:
        x = refs[0][...]
    else:
        x = jnp.where(pl.program_id(0) < n_ctx_tiles, refs[0][...], refs[1][...])
    h = _mx(_norm_mod(x, g_ref[...], mod_ref[0, k0:k0 + 1, :], mod_ref[0, k0 + 1:k0 + 2, :]))
    acc = jnp.zeros(x.shape, F32)
    for c in range(n_chunks):
        a = _dot(h, win_ref[0, c])
        g = _dot(h, win_ref[1, c])
        acc = acc + _dot(_mx(_silu(a) * g), wout_ref[c])
    y = x + (MACARON_W * mod_ref[0, k0 + 2:k0 + 3, :]) * acc
    if final:
        y = y * lax.rsqrt(jnp.mean(y * y, axis=-1, keepdims=True) + EPS) * refs[n_src + 4][...]
    o_ref[...] = y


def _ffn(rows, x, mod, g, w_in, w_out, k0, final_g=None):
    split = isinstance(x, tuple)
    d = x[0].shape[1] if split else x.shape[1]
    n_chunks, fc = w_out.shape[0], w_out.shape[1]
    tm = rows.tile(FFN_ROW_TILE)
    n_ctx_tiles = rows.b * rows.c // tm
    first = n_ctx_tiles if final_g is not None else 0
    mod_row = rows.mod_row(tm)
    if split and final_g is not None:
        split, srcs = False, [x[1]]
        src_specs = [pl.BlockSpec((tm, d), lambda i: (i, 0))]
    elif split:
        srcs = list(x)
        src_specs = [pl.BlockSpec((tm, d), lambda i: (jnp.minimum(i, n_ctx_tiles - 1), 0)),
                     pl.BlockSpec((tm, d), lambda i: (jnp.maximum(i - n_ctx_tiles, 0), 0))]
    else:
        srcs = [x]
        src_specs = [pl.BlockSpec((tm, d), lambda i: (i + first, 0))]
    extra, extra_specs = ([], []) if final_g is None else ([final_g], [pl.BlockSpec((1, d), lambda i: (0, 0))])
    return pl.pallas_call(
        functools.partial(_ffn_kernel, k0=k0, n_chunks=n_chunks, n_ctx_tiles=n_ctx_tiles if split else None,
                          final=final_g is not None),
        grid=(rows.total // tm - first,),
        in_specs=src_specs + [
            pl.BlockSpec((1, N_MOD, d), lambda i: (mod_row(i + first), 0, 0)),
            pl.BlockSpec((1, d), lambda i: (0, 0)),
            _resident((2, n_chunks, d, fc), lambda i: (0, 0, 0, 0)),
            _resident((n_chunks, fc, d), lambda i: (0, 0, 0)),
        ] + extra_specs,
        out_specs=pl.BlockSpec((tm, d), lambda i: (i, 0)),
        out_shape=jax.ShapeDtypeStruct((rows.total - first * tm, d), F32),
        compiler_params=_params(("parallel",)),
        name="ffn",
    )(*srcs, mod, g, w_in, w_out, *extra)


def _ffn_weights(w_in, w_out, fc):
    d, two_f = w_in.shape
    f = two_f // 2
    w_in = _mx(w_in).reshape(d, 2, f // fc, fc).transpose(1, 2, 0, 3)
    return w_in, _mx(w_out).reshape(f // fc, fc, d)


def _lru_in_kernel(x_ref, mod_ref, g_ref, w_ref, xb_ref, gate_ref, *, d_rnn):
    h = _mx(_norm_mod(x_ref[...], g_ref[...], mod_ref[0, 3:4, :], mod_ref[0, 4:5, :]))
    xb_ref[0] = _dot(h, w_ref[:, :d_rnn])
    gate_ref[...] = jax.nn.gelu(_dot(h, w_ref[:, d_rnn:])).astype(gate_ref.dtype)


def _lru_in(rows, x, mod, g, w_in):
    d = x.shape[1]
    d_rnn = w_in.shape[1] // 2
    tm = rows.tile(LRU_ROW_TILE)
    seq = rows.seq_of_tile(tm)
    return pl.pallas_call(
        functools.partial(_lru_in_kernel, d_rnn=d_rnn),
        grid=(rows.total // tm,),
        in_specs=[
            pl.BlockSpec((tm, d), lambda i: (i, 0)),
            pl.BlockSpec((1, N_MOD, d), lambda i, f=rows.mod_row(tm): (f(i), 0, 0)),
            pl.BlockSpec((1, d), lambda i: (0, 0)),
            _resident((d, 2 * d_rnn), lambda i: (0, 0)),
        ],
        out_specs=[pl.BlockSpec((1, tm, d_rnn), lambda i: seq(i) + (0,)),
                   pl.BlockSpec((tm, d_rnn), lambda i: (i, 0))],
        out_shape=[jax.ShapeDtypeStruct((rows.b, rows.c + rows.t, d_rnn), F32),
                   jax.ShapeDtypeStruct((rows.total, d_rnn), MXU_DTYPE)],
        compiler_params=_params(("parallel",)),
        name="lru_in",
    )(x, mod, g, w_in)


def _lru_scan_kernel(main_ref, prev_ref, next_ref, cw_ref, cb_ref, gw_ref, gb_ref, lam_ref, out_ref,
                     ext_ref, a_ref, b_ref, hs_ref, h_ref, *, ncc, ncl, pitch):
    bsz, tc, d_rnn = main_ref.shape
    d = pl.program_id(0)
    s = pl.program_id(1)
    n_slabs = d_rnn // LANES
    gw = d_rnn // LRU_GROUPS

    @pl.when(s == 0)
    def _():
        h_ref[...] = jnp.zeros(h_ref.shape, F32)

    q_rev = jnp.where(s < ncc, ncc - 1 - s, ncl - 1 - (s - ncc))
    q_fwd = jnp.where(s < ncc, s, s - ncc)
    q = jnp.where(d == 0, q_fwd, q_rev)
    n_seg = jnp.where(s < ncc, ncc, ncl)
    has_prev = (q > 0).astype(F32)
    has_next = (q < n_seg - 1).astype(F32)

    nl = -lam_ref[0]
    sp = jnp.maximum(nl, 0.0) + jnp.log1p(jnp.exp(-jnp.abs(nl)))
    half_rate = (-0.5 * LRU_C * math.log2(math.e)) * sp

    half = tc // 2
    base = SUBLANES - CONV_PAD_LEFT
    def conv(b, slab):
        ls = slice(slab * LANES, (slab + 1) * LANES)
        ext_ref[b, slab, 0:SUBLANES, :] = prev_ref[b, :, ls] * has_prev
        ext_ref[b, slab, SUBLANES:SUBLANES + tc, :] = main_ref[b, :, ls]
        ext_ref[b, slab, SUBLANES + tc:, :] = next_ref[b, :, ls] * has_next
        parities = []
        for par in range(2):
            acc = cb_ref[:, ls] + cw_ref[0:1, ls] * ext_ref[b, slab, pl.ds(base + par, half, stride=2), :]
            for k in range(1, CONV_W):
                acc = acc + cw_ref[k:k + 1, ls] * ext_ref[b, slab, pl.ds(base + par + k, half, stride=2), :]
            parities.append(acc)
        return jnp.concatenate(parities, axis=0)

    per_group = gw // LANES
    for b in range(bsz):
        for gi in range(LRU_GROUPS):
            xg = jnp.concatenate([conv(b, gi * per_group + j) for j in range(per_group)], axis=1)
            t = jnp.tanh(_dot(_mx(xg), gw_ref[0, gi]) + gb_ref[0, gi])
            rate = half_rate[:, gi * gw:(gi + 1) * gw]
            a = jnp.exp2(rate * t[:, :gw] + rate)
            root = jnp.exp2(0.5 * jnp.log2(1.0 - a * a))
            bb = root * ((0.5 * xg) * (t[:, gw:] + 1.0))
            for j in range(per_group):
                ls = slice(j * LANES, (j + 1) * LANES)
                for par in range(2):
                    rows_p = slice(par * half, (par + 1) * half)
                    dst = (gi * per_group + j, pl.ds(b * pitch + par, half, stride=2), slice(None))
                    a_ref[dst] = a[rows_p, ls]
                    b_ref[dst] = bb[rows_p, ls]

    def step(t, hs):
        row = t + d * (tc - 1 - 2 * t)
        new = []
        for slab in range(n_slabs):
            idx = (slab, pl.ds(row, bsz, stride=pitch), slice(None))
            h = a_ref[idx] * hs[slab] + b_ref[idx]
            hs_ref[idx] = h
            new.append(h)
        return tuple(new)

    hs = lax.fori_loop(0, tc, step, tuple(h_ref[slab] for slab in range(n_slabs)), unroll=8)
    for slab in range(n_slabs):
        h_ref[slab] = hs[slab]

    for b in range(bsz):
        for slab in range(n_slabs):
            out_ref[0, b, :, slab * LANES:(slab + 1) * LANES] = (
                hs_ref[slab, b * pitch:b * pitch + tc, :].astype(out_ref.dtype))


def _lru_scan(rows, xb, conv_w, conv_b, gate_w, gate_b, lam):
    bsz, n_seq, d_rnn = xb.shape
    tc = LRU_CHUNK
    ncc, ncl = rows.c // tc, rows.t // tc
    gw = d_rnn // LRU_GROUPS
    pitch = tc + SUBLANES
    per8 = tc // SUBLANES
    last8 = n_seq // SUBLANES - 1

    def chunk(d, s):
        rev = jnp.where(s < ncc, ncc - 1 - s, ncc + ncl - 1 - (s - ncc))
        return jnp.where(d == 0, s, rev)

    n_slabs = d_rnn // LANES
    return pl.pallas_call(
        functools.partial(_lru_scan_kernel, ncc=ncc, ncl=ncl, pitch=pitch),
        grid=(2, ncc + ncl),
        in_specs=[
            pl.BlockSpec((bsz, tc, d_rnn), lambda d, s: (0, chunk(d, s), 0)),
            pl.BlockSpec((bsz, SUBLANES, d_rnn), lambda d, s: (0, jnp.maximum(chunk(d, s) * per8 - 1, 0), 0)),
            pl.BlockSpec((bsz, SUBLANES, d_rnn),
                         lambda d, s: (0, jnp.minimum((chunk(d, s) + 1) * per8, last8), 0)),
            pl.BlockSpec((CONV_W, d_rnn), lambda d, s: (0, 0)),
            pl.BlockSpec((1, d_rnn), lambda d, s: (0, 0)),
            pl.BlockSpec((1, LRU_GROUPS, gw, 2 * gw), lambda d, s: (d, 0, 0, 0)),
            pl.BlockSpec((1, LRU_GROUPS, 1, 2 * gw), lambda d, s: (d, 0, 0, 0)),
            pl.BlockSpec((1, 1, d_rnn), lambda d, s: (d, 0, 0)),
        ],
        out_specs=pl.BlockSpec((1, bsz, tc, d_rnn), lambda d, s: (d, 0, chunk(d, s), 0)),
        out_shape=jax.ShapeDtypeStruct((2, bsz, n_seq, d_rnn), MXU_DTYPE),
        scratch_shapes=[
            pltpu.VMEM((bsz, n_slabs, tc + 2 * SUBLANES, LANES), F32),
            pltpu.VMEM((n_slabs, bsz * pitch, LANES), F32),
            pltpu.VMEM((n_slabs, bsz * pitch, LANES), F32),
            pltpu.VMEM((n_slabs, bsz * pitch, LANES), F32),
            pltpu.VMEM((n_slabs, bsz, LANES), F32),
        ],
        compiler_params=_params(("arbitrary", "arbitrary")),
        name="lru_scan",
    )(xb, xb, xb, conv_w, conv_b, gate_w, gate_b, lam)


def _lru_gate_weights(gate_w, gate_b):
    n_dir, n_gate, g, blk, _ = gate_w.shape
    per = g // LRU_GROUPS
    gw = per * blk
    w = gate_w.reshape(n_dir, n_gate, LRU_GROUPS, per, blk, blk)
    eye = jnp.eye(per, dtype=gate_w.dtype)
    dense = jnp.einsum('dkgpij,pq->dgpikqj', w, eye).reshape(n_dir, LRU_GROUPS, gw, n_gate * gw)
    bias = gate_b.reshape(n_dir, n_gate, LRU_GROUPS, gw).transpose(0, 2, 1, 3)
    return _mx(0.5 * dense), 0.5 * bias.reshape(n_dir, LRU_GROUPS, 1, n_gate * gw)


def _lru_out_kernel(hf_ref, hb_ref, gate_ref, x_ref, mod_ref, w_ref, o_ref):
    hsum = hf_ref[0, 0].astype(F32) + hb_ref[0, 0].astype(F32)
    y = _dot(_mx(hsum * gate_ref[...].astype(F32)), w_ref[...])
    o_ref[...] = x_ref[...] + mod_ref[0, 5:6, :] * y


def _lru_out(rows, hs, gate, x, mod, w_out):
    d = x.shape[1]
    d_rnn = gate.shape[1]
    tm = rows.tile(LRU_ROW_TILE)
    seq = rows.seq_of_tile(tm)
    return pl.pallas_call(
        _lru_out_kernel,
        grid=(rows.total // tm,),
        in_specs=[
            pl.BlockSpec((1, 1, tm, d_rnn), lambda i: (0,) + seq(i) + (0,)),
            pl.BlockSpec((1, 1, tm, d_rnn), lambda i: (1,) + seq(i) + (0,)),
            pl.BlockSpec((tm, d_rnn), lambda i: (i, 0)),
            pl.BlockSpec((tm, d), lambda i: (i, 0)),
            pl.BlockSpec((1, N_MOD, d), lambda i, f=rows.mod_row(tm): (f(i), 0, 0)),
            _resident((d_rnn, d), lambda i: (0, 0)),
        ],
        out_specs=pl.BlockSpec((tm, d), lambda i: (i, 0)),
        out_shape=jax.ShapeDtypeStruct(x.shape, F32),
        compiler_params=_params(("parallel",)),
        name="lru_out",
    )(hs, hs, gate, x, mod, w_out)


def _mixer_rglru(rows, x, mod, g, w_in, conv_w, conv_b, gate_w, gate_b, lam, w_out):
    xb, gate = _lru_in(rows, x, mod, g, _mx(w_in))
    gw_dense, gb = _lru_gate_weights(gate_w, gate_b)
    hs = _lru_scan(rows, xb, conv_w, conv_b.reshape(1, -1), gw_dense, gb,
                   lam.reshape(2, 1, -1))
    return _lru_out(rows, hs, gate, x, mod, _mx(w_out))


def _ret_in_kernel(x_ref, mod_ref, g_ref, w_ref, cos_ref, sin_ref, q_ref, k_ref, v_ref, sg_ref,
                   *, hdk, hdv, dk):
    h = _mx(_norm_mod(x_ref[...], g_ref[...], mod_ref[0, 3:4, :], mod_ref[0, 4:5, :]))
    cos, sin = cos_ref[...], sin_ref[...]
    half = dk // 2

    def rope(z):
        parts = []
        for hd in range(hdk // dk):
            z1 = z[:, hd * dk:hd * dk + half]
            z2 = z[:, hd * dk + half:(hd + 1) * dk]
            parts += [z1 * cos - z2 * sin, z1 * sin + z2 * cos]
        return jnp.concatenate(parts, axis=-1)

    q_ref[...] = rope(_dot(h, w_ref[:, :hdk])).astype(q_ref.dtype)
    k_ref[...] = (rope(_dot(h, w_ref[:, hdk:2 * hdk])) * (dk ** -0.5)).astype(k_ref.dtype)
    v_ref[...] = _dot(h, w_ref[:, 2 * hdk:2 * hdk + hdv]).astype(v_ref.dtype)
    sg_ref[...] = _silu(_dot(h, w_ref[:, 2 * hdk + hdv:])).astype(sg_ref.dtype)


def _ret_in(rows, x, mod, g, w_in, cos_t, sin_t):
    d = x.shape[1]
    hdk = d
    hdv = (w_in.shape[1] - 2 * hdk) // 2
    dk = hdk // RET_HEADS
    tm = rows.tile(512)
    pos = rows.pos_block(tm)
    row_spec = lambda n: pl.BlockSpec((tm, n), lambda i: (i, 0))
    return pl.pallas_call(
        functools.partial(_ret_in_kernel, hdk=hdk, hdv=hdv, dk=dk),
        grid=(rows.total // tm,),
        in_specs=[
            row_spec(d),
            pl.BlockSpec((1, N_MOD, d), lambda i, f=rows.mod_row(tm): (f(i), 0, 0)),
            pl.BlockSpec((1, d), lambda i: (0, 0)),
            _resident(w_in.shape, lambda i: (0, 0)),
            pl.BlockSpec((tm, dk // 2), lambda i: (pos(i), 0)),
            pl.BlockSpec((tm, dk // 2), lambda i: (pos(i), 0)),
        ],
        out_specs=[row_spec(hdk), row_spec(hdk), row_spec(hdv), row_spec(hdv)],
        out_shape=[jax.ShapeDtypeStruct((rows.total, n), MXU_DTYPE) for n in (hdk, hdk, hdv, hdv)],
        compiler_params=_params(("parallel",)),
        name="ret_in",
    )(x, mod, g, w_in, cos_t, sin_t)


def _ret_log_gamma(hd):
    return float(np.log1p(-np.exp2(np.float32(-5.0 - hd)), dtype=np.float32))


def _ret_state_step(q, k, v, s_ref, hd, lg, q_pow, k_pow):
    s_old = s_ref[hd]
    o = _dot(_mx(q.astype(F32) * jnp.exp(lg * q_pow)), _mx(s_old))
    kd = (k.astype(F32) * jnp.exp(lg * k_pow)).T
    s_ref[hd] = s_old * math.exp(lg * q.shape[0]) + _dot(_mx(kd), v)
    return o


def _ret_fwd_kernel(q_ref, k_ref, v_ref, o_ref, s_ref, *, dk, dv, tc):
    @pl.when(pl.program_id(1) == 0)
    def _():
        s_ref[...] = jnp.zeros(s_ref.shape, F32)

    row = lax.broadcasted_iota(jnp.int32, (tc, tc), 0)
    col = lax.broadcasted_iota(jnp.int32, (tc, tc), 1)
    dist = jnp.abs(row - col).astype(F32)
    idx = lax.broadcasted_iota(jnp.int32, (tc, 1), 0).astype(F32)
    for sub in range(q_ref.shape[0] // tc):
        rs = slice(sub * tc, (sub + 1) * tc)
        for hd in range(RET_HEADS):
            lg = _ret_log_gamma(hd)
            q = q_ref[rs, hd * dk:(hd + 1) * dk]
            k = k_ref[rs, hd * dk:(hd + 1) * dk]
            v = v_ref[rs, hd * dv:(hd + 1) * dv]
            scores = _dot_nt(q, k) * jnp.exp(lg * dist)
            o = _dot(_mx(scores), v)
            o = o + _ret_state_step(q, k, v, s_ref, hd, lg, idx + 1.0, (tc - 1.0) - idx)
            o_ref[rs, hd * dv:(hd + 1) * dv] = o.astype(o_ref.dtype)


def _ret_bwd_kernel(q_ref, k_ref, v_ref, of_ref, sg_ref, x_ref, mod_ref, w_ref, o_ref, s_ref, y_ref,
                    *, dk, dv, tc):
    @pl.when(pl.program_id(1) == 0)
    def _():
        s_ref[...] = jnp.zeros(s_ref.shape, F32)

    idx = lax.broadcasted_iota(jnp.int32, (tc, 1), 0).astype(F32)
    for sub in reversed(range(q_ref.shape[0] // tc)):
        rs = slice(sub * tc, (sub + 1) * tc)
        for hd in range(RET_HEADS):
            lg = _ret_log_gamma(hd)
            q = q_ref[rs, hd * dk:(hd + 1) * dk]
            k = k_ref[rs, hd * dk:(hd + 1) * dk]
            v = v_ref[rs, hd * dv:(hd + 1) * dv]
            o = (of_ref[rs, hd * dv:(hd + 1) * dv].astype(F32)
                 + _ret_state_step(q, k, v, s_ref, hd, lg, tc - idx, idx))
            on = o * lax.rsqrt(jnp.mean(o * o, axis=-1, keepdims=True) + EPS)
            y_ref[rs, hd * dv:(hd + 1) * dv] = _mx(on * sg_ref[rs, hd * dv:(hd + 1) * dv].astype(F32))
    o_ref[...] = x_ref[...] + mod_ref[0, 5:6, :] * _dot(y_ref[...], w_ref[...])


def _ret_core(rows, q, k, v, sg, x, mod, w_out):
    d = x.shape[1]
    hdk, hdv = q.shape[1], v.shape[1]
    dk, dv = hdk // RET_HEADS, hdv // RET_HEADS
    tb = RET_BLOCK
    assert tb % RET_CHUNK == 0 and rows.c % tb == 0 and rows.t % tb == 0
    n_steps = (rows.c + rows.t) // tb
    ncc = rows.c // tb
    bsz = rows.b
    fwd, bwd = rows.seq_block(tb, False), rows.seq_block(tb, True)
    seq = lambda f, n: pl.BlockSpec((tb, n), lambda b, s: (f(b, s), 0))
    o_part = pl.pallas_call(
        functools.partial(_ret_fwd_kernel, dk=dk, dv=dv, tc=RET_CHUNK),
        grid=(bsz, n_steps),
        in_specs=[seq(fwd, hdk), seq(fwd, hdk), seq(fwd, hdv)],
        out_specs=seq(fwd, hdv),
        out_shape=jax.ShapeDtypeStruct((rows.total, hdv), MXU_DTYPE),
        scratch_shapes=[pltpu.VMEM((RET_HEADS, dk, dv), F32)],
        compiler_params=_params(("parallel", "arbitrary")),
        name="ret_fwd",
    )(q, k, v)
    return pl.pallas_call(
        functools.partial(_ret_bwd_kernel, dk=dk, dv=dv, tc=RET_CHUNK),
        grid=(bsz, n_steps),
        in_specs=[seq(bwd, hdk), seq(bwd, hdk), seq(bwd, hdv), seq(bwd, hdv), seq(bwd, hdv), seq(bwd, d),
                  pl.BlockSpec((1, N_MOD, d), lambda b, s: (jnp.where(s < ncc, bsz, b), 0, 0)),
                  _resident((hdv, d), lambda b, s: (0, 0))],
        out_specs=seq(bwd, d),
        out_shape=jax.ShapeDtypeStruct(x.shape, F32),
        scratch_shapes=[pltpu.VMEM((RET_HEADS, dk, dv), F32), pltpu.VMEM((tb, hdv), MXU_DTYPE)],
        compiler_params=_params(("parallel", "arbitrary")),
        name="ret_bwd",
    )(q, k, v, o_part, sg, x, mod, w_out)


def _retention_tables(rows, tm, dk):
    theta = RET_THETA_BASE ** (-jnp.linspace(0.0, 1.0, dk // 2, dtype=F32))
    ang = jnp.arange(rows.t, dtype=F32)[:, None] * theta
    ones, zeros = jnp.ones((tm, dk // 2), F32), jnp.zeros((tm, dk // 2), F32)
    return jnp.concatenate([ones, jnp.cos(ang)]), jnp.concatenate([zeros, jnp.sin(ang)])


def _mixer_retention(rows, x, mod, g, w_in, w_out):
    tm = rows.tile(512)
    cos_t, sin_t = _retention_tables(rows, tm, x.shape[1] // RET_HEADS)
    q, k, v, sg = _ret_in(rows, x, mod, g, _mx(w_in), cos_t, sin_t)
    return _ret_core(rows, q, k, v, sg, x, mod, _mx(w_out))


def _dif_in_kernel(x_ref, mod_ref, g_ref, w_ref, cos_ref, sin_ref, q_ref, k_ref, vt_ref, *, dh):
    d = x_ref.shape[1]
    h = _mx(_norm_mod(x_ref[...], g_ref[...], mod_ref[0, 3:4, :], mod_ref[0, 4:5, :]))
    cos, sin = cos_ref[...], sin_ref[...]
    lane = lax.broadcasted_iota(jnp.int32, cos.shape, 1)
    first_half = (lane % dh) < (dh // 2)

    def rope(zj):
        swapped = jnp.where(first_half, pltpu.roll(zj, LANES - dh // 2, 1), pltpu.roll(zj, dh // 2, 1))
        return zj * cos + swapped * sin

    q = _dot(h, w_ref[:, :d])
    k = _dot(h, w_ref[:, d:2 * d])
    v = _dot(h, w_ref[:, 2 * d:])
    first_map = lane < dh
    for hd in range(d // LANES):
        qh = rope(q[:, hd * LANES:(hd + 1) * LANES]) * (dh ** -0.5 * math.log2(math.e))
        q_ref[2 * hd] = jnp.where(first_map, qh, 0.0).astype(q_ref.dtype)
        q_ref[2 * hd + 1] = jnp.where(first_map, 0.0, qh).astype(q_ref.dtype)
        k_ref[0, hd] = rope(k[:, hd * LANES:(hd + 1) * LANES]).astype(k_ref.dtype)
        vt_ref[0, hd] = v[:, hd * LANES:(hd + 1) * LANES].T.astype(vt_ref.dtype)


def _dif_in(rows, x, mod, g, w_in, cos_t, sin_t, dh):
    d = x.shape[1]
    heads = d // LANES
    n_seq = rows.c + rows.t
    tm = rows.tile(ATT_ROW_TILE)
    pos = rows.pos_block(tm)
    seq = rows.seq_of_tile(tm)
    return pl.pallas_call(
        functools.partial(_dif_in_kernel, dh=dh),
        grid=(rows.total // tm,),
        in_specs=[
            pl.BlockSpec((tm, d), lambda i: (i, 0)),
            pl.BlockSpec((1, N_MOD, d), lambda i, f=rows.mod_row(tm): (f(i), 0, 0)),
            pl.BlockSpec((1, d), lambda i: (0, 0)),
            _resident(w_in.shape, lambda i: (0, 0)),
            pl.BlockSpec((tm, LANES), lambda i: (pos(i), 0)),
            pl.BlockSpec((tm, LANES), lambda i: (pos(i), 0)),
        ],
        out_specs=[
            pl.BlockSpec((2 * heads, tm, LANES), lambda i: (0, i, 0)),
            pl.BlockSpec((1, heads, tm, LANES), lambda i: (seq(i)[0], 0, seq(i)[1], 0)),
            pl.BlockSpec((1, heads, LANES, tm), lambda i: (seq(i)[0], 0, 0, seq(i)[1])),
        ],
        out_shape=[
            jax.ShapeDtypeStruct((2 * heads, rows.total, LANES), MXU_DTYPE),
            jax.ShapeDtypeStruct((rows.b, heads, n_seq, LANES), MXU_DTYPE),
            jax.ShapeDtypeStruct((rows.b, heads, LANES, n_seq), MXU_DTYPE),
        ],
        compiler_params=_params(("parallel",)),
        name="dif_in",
    )(x, mod, g, w_in, cos_t, sin_t)


def _dif_attn_kernel(q_ref, k_ref, vt_ref, x_ref, mod_ref, lam_ref, sub_ref, w_ref, o_ref,
                     m_ref, l_ref, acc_ref, y_ref, s0_ref, s1_ref, mc0_ref, mc1_ref, *, dh, lam_init):
    j = pl.program_id(2)
    dv = 2 * dh
    n_heads = k_ref.shape[1]

    @pl.when(j == 0)
    def _():
        m_ref[...] = jnp.full(m_ref.shape, -jnp.inf, F32)
        l_ref[...] = jnp.zeros(l_ref.shape, F32)
        acc_ref[...] = jnp.zeros(acc_ref.shape, F32)

    def scores(hd, col, s_ref, mc_ref):
        st = _dot_nt(k_ref[0, hd], q_ref[col])
        s_ref[...] = st
        mc_ref[...] = jnp.max(st, axis=0, keepdims=True)

    def accumulate(hd, col, s_ref, mc_ref):
        m_old = m_ref[col]
        m_new = jnp.maximum(m_old, mc_ref[...])
        alpha = jnp.exp2(m_old - m_new)
        pt = jnp.exp2(s_ref[...] - m_new)
        l_ref[col] = alpha * l_ref[col] + jnp.sum(pt, axis=0, keepdims=True)
        acc_ref[col] = alpha * acc_ref[col] + _dot(vt_ref[0, hd], _mx(pt))
        m_ref[col] = m_new

    scores(0, 0, s0_ref, mc0_ref)

    def head(hd, carry):
        scores(hd, 2 * hd + 1, s1_ref, mc1_ref)
        accumulate(hd, 2 * hd, s0_ref, mc0_ref)
        nxt = jnp.minimum(hd + 1, n_heads - 1)
        scores(nxt, 2 * nxt, s0_ref, mc0_ref)
        accumulate(hd, 2 * hd + 1, s1_ref, mc1_ref)
        return carry

    lax.fori_loop(0, n_heads, head, 0, unroll=ATT_HEAD_UNROLL)

    @pl.when(j == pl.num_programs(2) - 1)
    def _():
        lp = lam_ref[...]
        lam = (jnp.exp(jnp.sum(lp[0:1] * lp[1:2], axis=-1, keepdims=True))
               - jnp.exp(jnp.sum(lp[2:3] * lp[3:4], axis=-1, keepdims=True)) + lam_init)
        for hd in range(n_heads):
            ot = acc_ref[2 * hd] / l_ref[2 * hd] - lam * (acc_ref[2 * hd + 1] / l_ref[2 * hd + 1])
            o = ot.T
            on = o * lax.rsqrt(jnp.mean(o * o, axis=-1, keepdims=True) + EPS) * sub_ref[...]
            y_ref[:, hd * dv:(hd + 1) * dv] = _mx(on * (1.0 - lam_init))
        o_ref[...] = x_ref[...] + mod_ref[0, 5:6, :] * _dot(y_ref[...], w_ref[...])


def _largest_tile(n, cap):
    return max(t for t in range(LANES, min(n, cap) + 1, LANES) if n % t == 0)


def _dif_attn(rows, q, k, vt, x, mod, lam_p, subln, w_out, lam_init, dh, ctx_queries):
    d = x.shape[1]
    bsz = rows.b
    if ctx_queries:
        tq = math.gcd(ATT_TQ, rows.c)
        tk = _largest_tile(rows.c, ATT_TK)
        nq, n_kv, q0, mod_row = rows.c // tq, rows.c // tk, 0, (lambda b: bsz)
    else:
        tq = math.gcd(ATT_TQ, rows.t)
        tk = _largest_tile(rows.c + rows.t, ATT_TK)
        nq, n_kv, q0, mod_row = rows.t // tq, (rows.c + rows.t) // tk, bsz * rows.c // tq, (lambda b: b)
    heads, dv = k.shape[1], k.shape[3]
    x_spec = pl.BlockSpec((tq, d), lambda b, i, j: (q0 + b * nq + i, 0))
    return pl.pallas_call(
        functools.partial(_dif_attn_kernel, dh=dh, lam_init=lam_init),
        grid=(bsz, nq, n_kv),
        in_specs=[
            pl.BlockSpec((2 * heads, tq, dv), lambda b, i, j: (0, q0 + b * nq + i, 0)),
            pl.BlockSpec((1, heads, tk, dv), lambda b, i, j: (b, 0, j, 0)),
            pl.BlockSpec((1, heads, dv, tk), lambda b, i, j: (b, 0, 0, j)),
            x_spec,
            pl.BlockSpec((1, N_MOD, d), lambda b, i, j: (mod_row(b), 0, 0)),
            pl.BlockSpec(lam_p.shape, lambda b, i, j: (0, 0)),
            pl.BlockSpec(subln.shape, lambda b, i, j: (0, 0)),
            _resident(w_out.shape, lambda b, i, j: (0, 0)),
        ],
        out_specs=pl.BlockSpec((tq, d), lambda b, i, j: (b * nq + i, 0)),
        out_shape=jax.ShapeDtypeStruct((bsz * nq * tq, d), F32),
        scratch_shapes=[
            pltpu.VMEM((2 * heads, 1, tq), F32),
            pltpu.VMEM((2 * heads, 1, tq), F32),
            pltpu.VMEM((2 * heads, dv, tq), F32),
            pltpu.VMEM((tq, d), MXU_DTYPE),
            pltpu.VMEM((tk, tq), F32),
            pltpu.VMEM((tk, tq), F32),
            pltpu.VMEM((1, tq), F32),
            pltpu.VMEM((1, tq), F32),
        ],
        compiler_params=_params(("parallel", "parallel", "arbitrary")),
        name="dif_attn_ctx" if ctx_queries else "dif_attn",
    )(q, k, vt, x, mod, lam_p, subln, w_out)


def _diff_tables(rows, tm, dh):
    n_freq = dh // 4
    inv = ROPE_BASE ** (-jnp.arange(n_freq, dtype=F32) / n_freq)
    n_rows = rows.t // GRID_W
    row = jnp.broadcast_to(jnp.arange(n_rows, dtype=F32)[:, None], (n_rows, GRID_W)).reshape(-1)
    col = jnp.broadcast_to(jnp.arange(GRID_W, dtype=F32)[None, :], (n_rows, GRID_W)).reshape(-1)
    ang = jnp.concatenate([row[:, None] * inv, col[:, None] * inv], axis=-1)
    cos, sin = jnp.cos(ang), jnp.sin(ang)
    reps = LANES // dh
    cos_l = jnp.tile(jnp.concatenate([cos, cos], axis=-1), (1, reps))
    sin_l = jnp.tile(jnp.concatenate([-sin, sin], axis=-1), (1, reps))
    ones, zeros = jnp.ones((tm, LANES), F32), jnp.zeros((tm, LANES), F32)
    return jnp.concatenate([ones, cos_l]), jnp.concatenate([zeros, sin_l])


def _mixer_diff(rows, x, mod, g, w_in, lam_p, subln, w_out, lam_init, need_ctx_out):
    d = x.shape[1]
    dh = d // DIFF_HEADS // 2
    cos_t, sin_t = _diff_tables(rows, rows.tile(ATT_ROW_TILE), dh)
    q, k, vt = _dif_in(rows, x, mod, g, _mx(w_in), cos_t, sin_t, dh)
    args = (mod, lam_p, subln.reshape(1, -1), _mx(w_out), lam_init, dh)
    x_lat = _dif_attn(rows, q, k, vt, x, *args, ctx_queries=False)
    x_ctx = _dif_attn(rows, q, k, vt, x, *args, ctx_queries=True) if need_ctx_out else x[:rows.b * rows.c]
    return x_ctx, x_lat


def kernel(x, c, ctx, c_ctx, norm_g, mod_w, mod_b, ffn_w_in, ffn_w_out, lru_w_in, lru_conv_w, lru_conv_b,
           lru_gate_w, lru_gate_b, lru_lam, lru_w_out, ret_w_in, ret_w_out, dif_w_in, dif_lam, dif_subln,
           dif_w_out, final_g):
    bsz, n_lat, d = x.shape
    n_ctx = ctx.shape[1]
    depth = norm_g.shape[0]
    assert bsz + 1 <= SUBLANES
    rows = _Rows(bsz, n_ctx, n_lat)

    cond = jnp.zeros((SUBLANES, d), F32).at[:bsz].set(c).at[bsz].set(c_ctx)
    mods = _ada_mod(cond, mod_w, mod_b).reshape(depth, SUBLANES, N_MOD, d)
    xs = (ctx.reshape(bsz * n_ctx, d), x.reshape(bsz * n_lat, d))

    fc = 256
    for i in range(depth):
        last = i == depth - 1
        mod = mods[i]
        g = norm_g[i].reshape(3, 1, d)
        xs = _ffn(rows, xs, mod, g[0], *_ffn_weights(ffn_w_in[i, 0], ffn_w_out[i, 0], fc), k0=0)
        kind, j = i % 3, i // 3
        if kind == 0:
            xs = _mixer_rglru(rows, xs, mod, g[1], lru_w_in[j], lru_conv_w[j], lru_conv_b[j], lru_gate_w[j],
                              lru_gate_b[j], lru_lam[j], lru_w_out[j])
        elif kind == 1:
            xs = _mixer_retention(rows, xs, mod, g[1], ret_w_in[j], ret_w_out[j])
        else:
            lam_init = 0.8 - 0.6 * math.exp(-0.3 * i)
            xs = _mixer_diff(rows, xs, mod, g[1], dif_w_in[j], dif_lam[j], dif_subln[j], dif_w_out[j],
                             lam_init, not last)
        xs = _ffn(rows, xs, mod, g[2], *_ffn_weights(ffn_w_in[i, 1], ffn_w_out[i, 1], fc), k0=6,
                  final_g=final_g.reshape(1, d) if last else None)
    return xs.reshape(bsz, n_lat, d)
```

```python
import functools
import math

import numpy as np
import jax
import jax.numpy as jnp
from jax import lax
from jax.experimental import pallas as pl
from jax.experimental.pallas import tpu as pltpu

F32 = jnp.float32
MXU_DTYPE = jnp.bfloat16

EPS = 1e-6
N_MOD = 9
ADA_TILE = 2304
MACARON_W = 0.5
FFN_ROW_TILE = 1024
FFN_CHUNK = 256
GRID_W = 64
ROPE_BASE = 10000.0
LRU_BLOCKS = 8
LRU_GROUPS = 2
CONV_W = 4
CONV_PAD_LEFT = 1
LRU_C = 8.0
LRU_CHUNK = 128
LRU_ROW_TILE = 256
RET_HEADS = 4
RET_CHUNK = 128
RET_BLOCK = 256
RET_THETA_BASE = 10000.0
DIFF_HEADS = 8
ATT_TQ = 512
ATT_TK = 1408
ATT_ROW_TILE = 256
ATT_HEAD_UNROLL = 4

LANES = 128
SUBLANES = 8
VMEM_LIMIT = 56 * 1024 * 1024


def _params(sem):
    return pltpu.CompilerParams(dimension_semantics=sem, vmem_limit_bytes=VMEM_LIMIT)


def _resident(shape, index_map):
    return pl.BlockSpec(shape, index_map, pipeline_mode=pl.Buffered(1))


def _mx(v):
    return v.astype(MXU_DTYPE)


def _dot(a, b):
    return jnp.dot(a, b, preferred_element_type=F32)


def _dot_nt(a, b):
    return lax.dot_general(a, b, (((1,), (1,)), ((), ())), preferred_element_type=F32)


def _norm_mod(x, g, shift, scale):
    ms = jnp.mean(x * x, axis=-1, keepdims=True)
    return (x * lax.rsqrt(ms + EPS)) * g * (1.0 + scale) + shift


def _silu(v):
    return v * jax.nn.sigmoid(v)


class _Rows:
    def __init__(self, bsz, n_ctx, n_lat):
        self.b, self.c, self.t = bsz, n_ctx, n_lat
        self.total = bsz * (n_ctx + n_lat)

    def tile(self, cap):
        return math.gcd(math.gcd(self.b * self.c, self.t), cap)

    def mod_row(self, tm):
        n_ctx_tiles = self.b * self.c // tm
        per_batch = self.t // tm
        bsz = self.b
        return lambda i: jnp.where(i < n_ctx_tiles, bsz, (i - n_ctx_tiles) // per_batch)

    def pos_block(self, tm):
        n_ctx_tiles = self.b * self.c // tm
        per_batch = self.t // tm
        return lambda i: jnp.where(i < n_ctx_tiles, 0, 1 + (i - n_ctx_tiles) % per_batch)

    def seq_of_tile(self, tm):
        n_ctx_tiles = self.b * self.c // tm
        ctx_per, lat_per = self.c // tm, self.t // tm

        def f(i):
            r = i - n_ctx_tiles
            is_ctx = i < n_ctx_tiles
            return (jnp.where(is_ctx, i // ctx_per, r // lat_per),
                    jnp.where(is_ctx, i % ctx_per, ctx_per + r % lat_per))
        return f

    def seq_block(self, chunk, reverse):
        ncc, ncl = self.c // chunk, self.t // chunk
        bsz = self.b

        def f(b, s):
            if reverse:
                q_ctx, q_lat = ncc - 1 - s, ncl - 1 - (s - ncc)
            else:
                q_ctx, q_lat = s, s - ncc
            return jnp.where(s < ncc, b * ncc + q_ctx, bsz * ncc + b * ncl + q_lat)
        return f


def _ada_kernel(c_ref, w_ref, b_ref, o_ref):
    s = _mx(_silu(c_ref[...]))
    o_ref[0] = _dot(s, _mx(w_ref[0])) + b_ref[0]


def _ada_mod(cond, mod_w, mod_b):
    depth, d, n = mod_w.shape
    tn = _largest_tile(n, ADA_TILE)
    return pl.pallas_call(
        _ada_kernel,
        grid=(depth, n // tn),
        in_specs=[
            pl.BlockSpec((SUBLANES, d), lambda l, j: (0, 0)),
            pl.BlockSpec((1, d, tn), lambda l, j: (l, 0, j)),
            pl.BlockSpec((1, 1, tn), lambda l, j: (l, 0, j)),
        ],
        out_specs=pl.BlockSpec((1, SUBLANES, tn), lambda l, j: (l, 0, j)),
        out_shape=jax.ShapeDtypeStruct((depth, SUBLANES, n), F32),
        compiler_params=_params(("parallel", "parallel")),
        name="ada_mod",
    )(cond, mod_w, mod_b.reshape(depth, 1, n))


def _ffn_kernel(*refs, k0, fc, n_ctx_tiles, final):
    n_src = 1 if n_ctx_tiles is None else 2
    mod_ref, g_ref, win_ref, wout_ref = refs[n_src:n_src + 4]
    o_ref = refs[-1]
    d_ff = wout_ref.shape[2]
    n_chunks = d_ff // fc
    if n_ctx_tiles is None:
        x = refs[0][...]
    else:
        x = jnp.where(pl.program_id(0) < n_ctx_tiles, refs[0][...], refs[1][...])
    h = _mx(_norm_mod(x, g_ref[...], mod_ref[0, k0:k0 + 1, :], mod_ref[0, k0 + 1:k0 + 2, :]))
    acc = jnp.zeros(x.shape, F32)
    for c in range(n_chunks):
        cols = slice(c * fc, (c + 1) * fc)
        a = _dot(h, win_ref[0, 0, :, cols])
        g = _dot(h, win_ref[0, 0, :, d_ff + c * fc:d_ff + (c + 1) * fc])
        acc = acc + _dot(_mx(_silu(a) * g), wout_ref[0, 0, cols, :])
    y = x + (MACARON_W * mod_ref[0, k0 + 2:k0 + 3, :]) * acc
    if final:
        y = y * lax.rsqrt(jnp.mean(y * y, axis=-1, keepdims=True) + EPS) * refs[n_src + 4][...]
    o_ref[...] = y


def _ffn(rows, x, mod, g, w_in, w_out, layer, sub, final_g=None):
    split = isinstance(x, tuple)
    d = x[0].shape[1] if split else x.shape[1]
    k0 = 6 * sub
    d_ff = w_out.shape[2]
    assert d_ff % FFN_CHUNK == 0 and FFN_CHUNK % LANES == 0
    tm = rows.tile(FFN_ROW_TILE)
    n_ctx_tiles = rows.b * rows.c // tm
    first = n_ctx_tiles if final_g is not None else 0
    mod_row = rows.mod_row(tm)
    if split and final_g is not None:
        split, srcs = False, [x[1]]
        src_specs = [pl.BlockSpec((tm, d), lambda i: (i, 0))]
    elif split:
        srcs = list(x)
        src_specs = [pl.BlockSpec((tm, d), lambda i: (jnp.minimum(i, n_ctx_tiles - 1), 0)),
                     pl.BlockSpec((tm, d), lambda i: (jnp.maximum(i - n_ctx_tiles, 0), 0))]
    else:
        srcs = [x]
        src_specs = [pl.BlockSpec((tm, d), lambda i: (i + first, 0))]
    extra, extra_specs = ([], []) if final_g is None else ([final_g], [pl.BlockSpec((1, d), lambda i: (0, 0))])
    return pl.pallas_call(
        functools.partial(_ffn_kernel, k0=k0, fc=FFN_CHUNK, n_ctx_tiles=n_ctx_tiles if split else None,
                          final=final_g is not None),
        grid=(rows.total // tm - first,),
        in_specs=src_specs + [
            pl.BlockSpec((1, N_MOD, d), lambda i: (mod_row(i + first), 0, 0)),
            pl.BlockSpec((1, d), lambda i: (0, 0)),
            _resident((1, 1, d, 2 * d_ff), lambda i: (layer, sub, 0, 0)),
            _resident((1, 1, d_ff, d), lambda i: (layer, sub, 0, 0)),
        ] + extra_specs,
        out_specs=pl.BlockSpec((tm, d), lambda i: (i, 0)),
        out_shape=jax.ShapeDtypeStruct((rows.total - first * tm, d), F32),
        compiler_params=_params(("parallel",)),
        name="ffn",
    )(*srcs, mod, g, w_in, w_out, *extra)


def _lru_in_kernel(x_ref, mod_ref, g_ref, w_ref, xb_ref, gate_ref, *, d_rnn):
    h = _mx(_norm_mod(x_ref[...], g_ref[...], mod_ref[0, 3:4, :], mod_ref[0, 4:5, :]))
    xb_ref[0] = _dot(h, w_ref[:, :d_rnn])
    gate_ref[...] = jax.nn.gelu(_dot(h, w_ref[:, d_rnn:])).astype(gate_ref.dtype)


def _lru_in(rows, x, mod, g, w_in):
    d = x.shape[1]
    d_rnn = w_in.shape[1] // 2
    tm = rows.tile(LRU_ROW_TILE)
    seq = rows.seq_of_tile(tm)
    return pl.pallas_call(
        functools.partial(_lru_in_kernel, d_rnn=d_rnn),
        grid=(rows.total // tm,),
        in_specs=[
            pl.BlockSpec((tm, d), lambda i: (i, 0)),
            pl.BlockSpec((1, N_MOD, d), lambda i, f=rows.mod_row(tm): (f(i), 0, 0)),
            pl.BlockSpec((1, d), lambda i: (0, 0)),
            _resident((d, 2 * d_rnn), lambda i: (0, 0)),
        ],
        out_specs=[pl.BlockSpec((1, tm, d_rnn), lambda i: seq(i) + (0,)),
                   pl.BlockSpec((tm, d_rnn), lambda i: (i, 0))],
        out_shape=[jax.ShapeDtypeStruct((rows.b, rows.c + rows.t, d_rnn), F32),
                   jax.ShapeDtypeStruct((rows.total, d_rnn), MXU_DTYPE)],
        compiler_params=_params(("parallel",)),
        name="lru_in",
    )(x, mod, g, w_in)


def _lru_scan_kernel(main_ref, prev_ref, next_ref, cw_ref, cb_ref, gw_ref, gb_ref, lam_ref, out_ref,
                     ext_ref, a_ref, b_ref, hs_ref, h_ref, *, ncc, ncl, pitch):
    bsz, tc, d_rnn = main_ref.shape
    d = pl.program_id(0)
    s = pl.program_id(1)
    n_slabs = d_rnn // LANES
    gw = d_rnn // LRU_GROUPS

    @pl.when(s == 0)
    def _():
        h_ref[...] = jnp.zeros(h_ref.shape, F32)

    q_rev = jnp.where(s < ncc, ncc - 1 - s, ncl - 1 - (s - ncc))
    q_fwd = jnp.where(s < ncc, s, s - ncc)
    q = jnp.where(d == 0, q_fwd, q_rev)
    n_seg = jnp.where(s < ncc, ncc, ncl)
    has_prev = (q > 0).astype(F32)
    has_next = (q < n_seg - 1).astype(F32)

    nl = -lam_ref[0]
    sp = jnp.maximum(nl, 0.0) + jnp.log1p(jnp.exp(-jnp.abs(nl)))
    half_rate = (-0.5 * LRU_C * math.log2(math.e)) * sp

    half = tc // 2
    base = SUBLANES - CONV_PAD_LEFT
    def conv(b, slab):
        ls = slice(slab * LANES, (slab + 1) * LANES)
        ext_ref[b, slab, 0:SUBLANES, :] = prev_ref[b, :, ls] * has_prev
        ext_ref[b, slab, SUBLANES:SUBLANES + tc, :] = main_ref[b, :, ls]
        ext_ref[b, slab, SUBLANES + tc:, :] = next_ref[b, :, ls] * has_next
        parities = []
        for par in range(2):
            acc = cb_ref[:, ls] + cw_ref[0:1, ls] * ext_ref[b, slab, pl.ds(base + par, half, stride=2), :]
            for k in range(1, CONV_W):
                acc = acc + cw_ref[k:k + 1, ls] * ext_ref[b, slab, pl.ds(base + par + k, half, stride=2), :]
            parities.append(acc)
        return jnp.concatenate(parities, axis=0)

    per_group = gw // LANES
    for b in range(bsz):
        for gi in range(LRU_GROUPS):
            xg = jnp.concatenate([conv(b, gi * per_group + j) for j in range(per_group)], axis=1)
            t = jnp.tanh(_dot(_mx(xg), gw_ref[0, gi]) + gb_ref[0, gi])
            rate = half_rate[:, gi * gw:(gi + 1) * gw]
            a = jnp.exp2(rate * t[:, :gw] + rate)
            root = jnp.exp2(0.5 * jnp.log2(1.0 - a * a))
            bb = root * ((0.5 * xg) * (t[:, gw:] + 1.0))
            for j in range(per_group):
                ls = slice(j * LANES, (j + 1) * LANES)
                for par in range(2):
                    rows_p = slice(par * half, (par + 1) * half)
                    dst = (gi * per_group + j, pl.ds(b * pitch + par, half, stride=2), slice(None))
                    a_ref[dst] = a[rows_p, ls]
                    b_ref[dst] = bb[rows_p, ls]

    def step(t, hs):
        row = t + d * (tc - 1 - 2 * t)
        new = []
        for slab in range(n_slabs):
            idx = (slab, pl.ds(row, bsz, stride=pitch), slice(None))
            h = a_ref[idx] * hs[slab] + b_ref[idx]
            hs_ref[idx] = h
            new.append(h)
        return tuple(new)

    hs = lax.fori_loop(0, tc, step, tuple(h_ref[slab] for slab in range(n_slabs)), unroll=8)
    for slab in range(n_slabs):
        h_ref[slab] = hs[slab]

    for b in range(bsz):
        for slab in range(n_slabs):
            out_ref[0, b, :, slab * LANES:(slab + 1) * LANES] = (
                hs_ref[slab, b * pitch:b * pitch + tc, :].astype(out_ref.dtype))


def _lru_scan(rows, xb, conv_w, conv_b, gate_w, gate_b, lam):
    bsz, n_seq, d_rnn = xb.shape
    tc = LRU_CHUNK
    ncc, ncl = rows.c // tc, rows.t // tc
    gw = d_rnn // LRU_GROUPS
    pitch = tc + SUBLANES
    per8 = tc // SUBLANES
    last8 = n_seq // SUBLANES - 1

    def chunk(d, s):
        rev = jnp.where(s < ncc, ncc - 1 - s, ncc + ncl - 1 - (s - ncc))
        return jnp.where(d == 0, s, rev)

    n_slabs = d_rnn // LANES
    return pl.pallas_call(
        functools.partial(_lru_scan_kernel, ncc=ncc, ncl=ncl, pitch=pitch),
        grid=(2, ncc + ncl),
        in_specs=[
            pl.BlockSpec((bsz, tc, d_rnn), lambda d, s: (0, chunk(d, s), 0)),
            pl.BlockSpec((bsz, SUBLANES, d_rnn), lambda d, s: (0, jnp.maximum(chunk(d, s) * per8 - 1, 0), 0)),
            pl.BlockSpec((bsz, SUBLANES, d_rnn),
                         lambda d, s: (0, jnp.minimum((chunk(d, s) + 1) * per8, last8), 0)),
            pl.BlockSpec((CONV_W, d_rnn), lambda d, s: (0, 0)),
            pl.BlockSpec((1, d_rnn), lambda d, s: (0, 0)),
            pl.BlockSpec((1, LRU_GROUPS, gw, 2 * gw), lambda d, s: (d, 0, 0, 0)),
            pl.BlockSpec((1, LRU_GROUPS, 1, 2 * gw), lambda d, s: (d, 0, 0, 0)),
            pl.BlockSpec((1, 1, d_rnn), lambda d, s: (d, 0, 0)),
        ],
        out_specs=pl.BlockSpec((1, bsz, tc, d_rnn), lambda d, s: (d, 0, chunk(d, s), 0)),
        out_shape=jax.ShapeDtypeStruct((2, bsz, n_seq, d_rnn), MXU_DTYPE),
        scratch_shapes=[
            pltpu.VMEM((bsz, n_slabs, tc + 2 * SUBLANES, LANES), F32),
            pltpu.VMEM((n_slabs, bsz * pitch, LANES), F32),
            pltpu.VMEM((n_slabs, bsz * pitch, LANES), F32),
            pltpu.VMEM((n_slabs, bsz * pitch, LANES), F32),
            pltpu.VMEM((n_slabs, bsz, LANES), F32),
        ],
        compiler_params=_params(("arbitrary", "arbitrary")),
        name="lru_scan",
    )(xb, xb, xb, conv_w, conv_b, gate_w, gate_b, lam)


def _lru_gate_weights(gate_w, gate_b):
    n_dir, n_gate, g, blk, _ = gate_w.shape
    per = g // LRU_GROUPS
    gw = per * blk
    w = gate_w.reshape(n_dir, n_gate, LRU_GROUPS, per, blk, blk)
    eye = jnp.eye(per, dtype=gate_w.dtype)
    dense = jnp.einsum('dkgpij,pq->dgpikqj', w, eye).reshape(n_dir, LRU_GROUPS, gw, n_gate * gw)
    bias = gate_b.reshape(n_dir, n_gate, LRU_GROUPS, gw).transpose(0, 2, 1, 3)
    return _mx(0.5 * dense), 0.5 * bias.reshape(n_dir, LRU_GROUPS, 1, n_gate * gw)


def _lru_out_kernel(hf_ref, hb_ref, gate_ref, x_ref, mod_ref, w_ref, o_ref):
    hsum = hf_ref[0, 0].astype(F32) + hb_ref[0, 0].astype(F32)
    y = _dot(_mx(hsum * gate_ref[...].astype(F32)), w_ref[...])
    o_ref[...] = x_ref[...] + mod_ref[0, 5:6, :] * y


def _lru_out(rows, hs, gate, x, mod, w_out):
    d = x.shape[1]
    d_rnn = gate.shape[1]
    tm = rows.tile(LRU_ROW_TILE)
    seq = rows.seq_of_tile(tm)
    return pl.pallas_call(
        _lru_out_kernel,
        grid=(rows.total // tm,),
        in_specs=[
            pl.BlockSpec((1, 1, tm, d_rnn), lambda i: (0,) + seq(i) + (0,)),
            pl.BlockSpec((1, 1, tm, d_rnn), lambda i: (1,) + seq(i) + (0,)),
            pl.BlockSpec((tm, d_rnn), lambda i: (i, 0)),
            pl.BlockSpec((tm, d), lambda i: (i, 0)),
            pl.BlockSpec((1, N_MOD, d), lambda i, f=rows.mod_row(tm): (f(i), 0, 0)),
            _resident((d_rnn, d), lambda i: (0, 0)),
        ],
        out_specs=pl.BlockSpec((tm, d), lambda i: (i, 0)),
        out_shape=jax.ShapeDtypeStruct(x.shape, F32),
        compiler_params=_params(("parallel",)),
        name="lru_out",
    )(hs, hs, gate, x, mod, w_out)


def _mixer_rglru(rows, x, mod, g, w_in, conv_w, conv_b, gate_w, gate_b, lam, w_out):
    xb, gate = _lru_in(rows, x, mod, g, _mx(w_in))
    gw_dense, gb = _lru_gate_weights(gate_w, gate_b)
    hs = _lru_scan(rows, xb, conv_w, conv_b.reshape(1, -1), gw_dense, gb,
                   lam.reshape(2, 1, -1))
    return _lru_out(rows, hs, gate, x, mod, _mx(w_out))


def _ret_in_kernel(x_ref, mod_ref, g_ref, w_ref, cos_ref, sin_ref, q_ref, k_ref, v_ref, sg_ref,
                   *, hdk, hdv, dk):
    h = _mx(_norm_mod(x_ref[...], g_ref[...], mod_ref[0, 3:4, :], mod_ref[0, 4:5, :]))
    cos, sin = cos_ref[...], sin_ref[...]
    half = dk // 2

    def rope(z):
        parts = []
        for hd in range(hdk // dk):
            z1 = z[:, hd * dk:hd * dk + half]
            z2 = z[:, hd * dk + half:(hd + 1) * dk]
            parts += [z1 * cos - z2 * sin, z1 * sin + z2 * cos]
        return jnp.concatenate(parts, axis=-1)

    q_ref[...] = rope(_dot(h, w_ref[:, :hdk])).astype(q_ref.dtype)
    k_ref[...] = (rope(_dot(h, w_ref[:, hdk:2 * hdk])) * (dk ** -0.5)).astype(k_ref.dtype)
    v_ref[...] = _dot(h, w_ref[:, 2 * hdk:2 * hdk + hdv]).astype(v_ref.dtype)
    sg_ref[...] = _silu(_dot(h, w_ref[:, 2 * hdk + hdv:])).astype(sg_ref.dtype)


def _ret_in(rows, x, mod, g, w_in, cos_t, sin_t):
    d = x.shape[1]
    hdk = d
    hdv = (w_in.shape[1] - 2 * hdk) // 2
    dk = hdk // RET_HEADS
    tm = rows.tile(512)
    pos = rows.pos_block(tm)
    row_spec = lambda n: pl.BlockSpec((tm, n), lambda i: (i, 0))
    return pl.pallas_call(
        functools.partial(_ret_in_kernel, hdk=hdk, hdv=hdv, dk=dk),
        grid=(rows.total // tm,),
        in_specs=[
            row_spec(d),
            pl.BlockSpec((1, N_MOD, d), lambda i, f=rows.mod_row(tm): (f(i), 0, 0)),
            pl.BlockSpec((1, d), lambda i: (0, 0)),
            _resident(w_in.shape, lambda i: (0, 0)),
            pl.BlockSpec((tm, dk // 2), lambda i: (pos(i), 0)),
            pl.BlockSpec((tm, dk // 2), lambda i: (pos(i), 0)),
        ],
        out_specs=[row_spec(hdk), row_spec(hdk), row_spec(hdv), row_spec(hdv)],
        out_shape=[jax.ShapeDtypeStruct((rows.total, n), MXU_DTYPE) for n in (hdk, hdk, hdv, hdv)],
        compiler_params=_params(("parallel",)),
        name="ret_in",
    )(x, mod, g, w_in, cos_t, sin_t)


def _ret_log_gamma(hd):
    return float(np.log1p(-np.exp2(np.float32(-5.0 - hd)), dtype=np.float32))


def _ret_state_step(q, k, v, s_ref, hd, lg, q_pow, k_pow):
    s_old = s_ref[hd]
    o = _dot(_mx(q.astype(F32) * jnp.exp(lg * q_pow)), _mx(s_old))
    kd = (k.astype(F32) * jnp.exp(lg * k_pow)).T
    s_ref[hd] = s_old * math.exp(lg * q.shape[0]) + _dot(_mx(kd), v)
    return o


def _ret_fwd_kernel(q_ref, k_ref, v_ref, o_ref, s_ref, *, dk, dv, tc):
    @pl.when(pl.program_id(1) == 0)
    def _():
        s_ref[...] = jnp.zeros(s_ref.shape, F32)

    row = lax.broadcasted_iota(jnp.int32, (tc, tc), 0)
    col = lax.broadcasted_iota(jnp.int32, (tc, tc), 1)
    dist = jnp.abs(row - col).astype(F32)
    idx = lax.broadcasted_iota(jnp.int32, (tc, 1), 0).astype(F32)
    for sub in range(q_ref.shape[0] // tc):
        rs = slice(sub * tc, (sub + 1) * tc)
        for hd in range(RET_HEADS):
            lg = _ret_log_gamma(hd)
            q = q_ref[rs, hd * dk:(hd + 1) * dk]
            k = k_ref[rs, hd * dk:(hd + 1) * dk]
            v = v_ref[rs, hd * dv:(hd + 1) * dv]
            scores = _dot_nt(q, k) * jnp.exp(lg * dist)
            o = _dot(_mx(scores), v)
            o = o + _ret_state_step(q, k, v, s_ref, hd, lg, idx + 1.0, (tc - 1.0) - idx)
            o_ref[rs, hd * dv:(hd + 1) * dv] = o.astype(o_ref.dtype)


def _ret_bwd_kernel(q_ref, k_ref, v_ref, of_ref, sg_ref, x_ref, mod_ref, w_ref, o_ref, s_ref, y_ref,
                    *, dk, dv, tc):
    @pl.when(pl.program_id(1) == 0)
    def _():
        s_ref[...] = jnp.zeros(s_ref.shape, F32)

    idx = lax.broadcasted_iota(jnp.int32, (tc, 1), 0).astype(F32)
    for sub in reversed(range(q_ref.shape[0] // tc)):
        rs = slice(sub * tc, (sub + 1) * tc)
        for hd in range(RET_HEADS):
            lg = _ret_log_gamma(hd)
            q = q_ref[rs, hd * dk:(hd + 1) * dk]
            k = k_ref[rs, hd * dk:(hd + 1) * dk]
            v = v_ref[rs, hd * dv:(hd + 1) * dv]
            o = (of_ref[rs, hd * dv:(hd + 1) * dv].astype(F32)
                 + _ret_state_step(q, k, v, s_ref, hd, lg, tc - idx, idx))
            on = o * lax.rsqrt(jnp.mean(o * o, axis=-1, keepdims=True) + EPS)
            y_ref[rs, hd * dv:(hd + 1) * dv] = _mx(on * sg_ref[rs, hd * dv:(hd + 1) * dv].astype(F32))
    o_ref[...] = x_ref[...] + mod_ref[0, 5:6, :] * _dot(y_ref[...], w_ref[...])


def _ret_core(rows, q, k, v, sg, x, mod, w_out):
    d = x.shape[1]
    hdk, hdv = q.shape[1], v.shape[1]
    dk, dv = hdk // RET_HEADS, hdv // RET_HEADS
    tb = RET_BLOCK
    assert tb % RET_CHUNK == 0 and rows.c % tb == 0 and rows.t % tb == 0
    n_steps = (rows.c + rows.t) // tb
    ncc = rows.c // tb
    bsz = rows.b
    fwd, bwd = rows.seq_block(tb, False), rows.seq_block(tb, True)
    seq = lambda f, n: pl.BlockSpec((tb, n), lambda b, s: (f(b, s), 0))
    o_part = pl.pallas_call(
        functools.partial(_ret_fwd_kernel, dk=dk, dv=dv, tc=RET_CHUNK),
        grid=(bsz, n_steps),
        in_specs=[seq(fwd, hdk), seq(fwd, hdk), seq(fwd, hdv)],
        out_specs=seq(fwd, hdv),
        out_shape=jax.ShapeDtypeStruct((rows.total, hdv), MXU_DTYPE),
        scratch_shapes=[pltpu.VMEM((RET_HEADS, dk, dv), F32)],
        compiler_params=_params(("parallel", "arbitrary")),
        name="ret_fwd",
    )(q, k, v)
    return pl.pallas_call(
        functools.partial(_ret_bwd_kernel, dk=dk, dv=dv, tc=RET_CHUNK),
        grid=(bsz, n_steps),
        in_specs=[seq(bwd, hdk), seq(bwd, hdk), seq(bwd, hdv), seq(bwd, hdv), seq(bwd, hdv), seq(bwd, d),
                  pl.BlockSpec((1, N_MOD, d), lambda b, s: (jnp.where(s < ncc, bsz, b), 0, 0)),
                  _resident((hdv, d), lambda b, s: (0, 0))],
        out_specs=seq(bwd, d),
        out_shape=jax.ShapeDtypeStruct(x.shape, F32),
        scratch_shapes=[pltpu.VMEM((RET_HEADS, dk, dv), F32), pltpu.VMEM((tb, hdv), MXU_DTYPE)],
        compiler_params=_params(("parallel", "arbitrary")),
        name="ret_bwd",
    )(q, k, v, o_part, sg, x, mod, w_out)


def _retention_tables(rows, tm, dk):
    theta = RET_THETA_BASE ** (-jnp.linspace(0.0, 1.0, dk // 2, dtype=F32))
    ang = jnp.arange(rows.t, dtype=F32)[:, None] * theta
    ones, zeros = jnp.ones((tm, dk // 2), F32), jnp.zeros((tm, dk // 2), F32)
    return jnp.concatenate([ones, jnp.cos(ang)]), jnp.concatenate([zeros, jnp.sin(ang)])


def _mixer_retention(rows, x, mod, g, w_in, w_out):
    tm = rows.tile(512)
    cos_t, sin_t = _retention_tables(rows, tm, x.shape[1] // RET_HEADS)
    q, k, v, sg = _ret_in(rows, x, mod, g, _mx(w_in), cos_t, sin_t)
    return _ret_core(rows, q, k, v, sg, x, mod, _mx(w_out))


def _dif_in_kernel(x_ref, mod_ref, g_ref, w_ref, cos_ref, sin_ref, q_ref, k_ref, vt_ref, *, dh):
    d = x_ref.shape[1]
    h = _mx(_norm_mod(x_ref[...], g_ref[...], mod_ref[0, 3:4, :], mod_ref[0, 4:5, :]))
    cos, sin = cos_ref[...], sin_ref[...]
    lane = lax.broadcasted_iota(jnp.int32, cos.shape, 1)
    first_half = (lane % dh) < (dh // 2)

    def rope(zj):
        swapped = jnp.where(first_half, pltpu.roll(zj, LANES - dh // 2, 1), pltpu.roll(zj, dh // 2, 1))
        return zj * cos + swapped * sin

    q = _dot(h, w_ref[:, :d])
    k = _dot(h, w_ref[:, d:2 * d])
    v = _dot(h, w_ref[:, 2 * d:])
    first_map = lane < dh
    for hd in range(d // LANES):
        qh = rope(q[:, hd * LANES:(hd + 1) * LANES]) * (dh ** -0.5 * math.log2(math.e))
        q_ref[2 * hd] = jnp.where(first_map, qh, 0.0).astype(q_ref.dtype)
        q_ref[2 * hd + 1] = jnp.where(first_map, 0.0, qh).astype(q_ref.dtype)
        k_ref[0, hd] = rope(k[:, hd * LANES:(hd + 1) * LANES]).astype(k_ref.dtype)
        vt_ref[0, hd] = v[:, hd * LANES:(hd + 1) * LANES].T.astype(vt_ref.dtype)


def _dif_in(rows, x, mod, g, w_in, cos_t, sin_t, dh):
    d = x.shape[1]
    heads = d // LANES
    n_seq = rows.c + rows.t
    tm = rows.tile(ATT_ROW_TILE)
    pos = rows.pos_block(tm)
    seq = rows.seq_of_tile(tm)
    return pl.pallas_call(
        functools.partial(_dif_in_kernel, dh=dh),
        grid=(rows.total // tm,),
        in_specs=[
            pl.BlockSpec((tm, d), lambda i: (i, 0)),
            pl.BlockSpec((1, N_MOD, d), lambda i, f=rows.mod_row(tm): (f(i), 0, 0)),
            pl.BlockSpec((1, d), lambda i: (0, 0)),
            _resident(w_in.shape, lambda i: (0, 0)),
            pl.BlockSpec((tm, LANES), lambda i: (pos(i), 0)),
            pl.BlockSpec((tm, LANES), lambda i: (pos(i), 0)),
        ],
        out_specs=[
            pl.BlockSpec((2 * heads, tm, LANES), lambda i: (0, i, 0)),
            pl.BlockSpec((1, heads, tm, LANES), lambda i: (seq(i)[0], 0, seq(i)[1], 0)),
            pl.BlockSpec((1, heads, LANES, tm), lambda i: (seq(i)[0], 0, 0, seq(i)[1])),
        ],
        out_shape=[
            jax.ShapeDtypeStruct((2 * heads, rows.total, LANES), MXU_DTYPE),
            jax.ShapeDtypeStruct((rows.b, heads, n_seq, LANES), MXU_DTYPE),
            jax.ShapeDtypeStruct((rows.b, heads, LANES, n_seq), MXU_DTYPE),
        ],
        compiler_params=_params(("parallel",)),
        name="dif_in",
    )(x, mod, g, w_in, cos_t, sin_t)


def _dif_attn_kernel(q_ref, k_ref, vt_ref, x_ref, mod_ref, lam_ref, sub_ref, w_ref, o_ref,
                     m_ref, l_ref, acc_ref, y_ref, s0_ref, s1_ref, mc0_ref, mc1_ref, *, dh, lam_init):
    j = pl.program_id(2)
    dv = 2 * dh
    n_heads = k_ref.shape[1]

    @pl.when(j == 0)
    def _():
        m_ref[...] = jnp.full(m_ref.shape, -jnp.inf, F32)
        l_ref[...] = jnp.zeros(l_ref.shape, F32)
        acc_ref[...] = jnp.zeros(acc_ref.shape, F32)

    def scores(hd, col, s_ref, mc_ref):
        st = _dot_nt(k_ref[0, hd], q_ref[col])
        s_ref[...] = st
        mc_ref[...] = jnp.max(st, axis=0, keepdims=True)

    def accumulate(hd, col, s_ref, mc_ref):
        m_old = m_ref[col]
        m_new = jnp.maximum(m_old, mc_ref[...])
        alpha = jnp.exp2(m_old - m_new)
        pt = jnp.exp2(s_ref[...] - m_new)
        l_ref[col] = alpha * l_ref[col] + jnp.sum(pt, axis=0, keepdims=True)
        acc_ref[col] = alpha * acc_ref[col] + _dot(vt_ref[0, hd], _mx(pt))
        m_ref[col] = m_new

    scores(0, 0, s0_ref, mc0_ref)

    def head(hd, carry):
        scores(hd, 2 * hd + 1, s1_ref, mc1_ref)
        accumulate(hd, 2 * hd, s0_ref, mc0_ref)
        nxt = jnp.minimum(hd + 1, n_heads - 1)
        scores(nxt, 2 * nxt, s0_ref, mc0_ref)
        accumulate(hd, 2 * hd + 1, s1_ref, mc1_ref)
        return carry

    lax.fori_loop(0, n_heads, head, 0, unroll=ATT_HEAD_UNROLL)

    @pl.when(j == pl.num_programs(2) - 1)
    def _():
        lp = lam_ref[...]
        lam = (jnp.exp(jnp.sum(lp[0:1] * lp[1:2], axis=-1, keepdims=True))
               - jnp.exp(jnp.sum(lp[2:3] * lp[3:4], axis=-1, keepdims=True)) + lam_init)
        for hd in range(n_heads):
            ot = acc_ref[2 * hd] / l_ref[2 * hd] - lam * (acc_ref[2 * hd + 1] / l_ref[2 * hd + 1])
            o = ot.T
            on = o * lax.rsqrt(jnp.mean(o * o, axis=-1, keepdims=True) + EPS) * sub_ref[...]
            y_ref[:, hd * dv:(hd + 1) * dv] = _mx(on * (1.0 - lam_init))
        o_ref[...] = x_ref[...] + mod_ref[0, 5:6, :] * _dot(y_ref[...], w_ref[...])


def _largest_tile(n, cap):
    return max(t for t in range(LANES, min(n, cap) + 1, LANES) if n % t == 0)


def _dif_attn(rows, q, k, vt, x, mod, lam_p, subln, w_out, lam_init, dh, ctx_queries):
    d = x.shape[1]
    bsz = rows.b
    if ctx_queries:
        tq = math.gcd(ATT_TQ, rows.c)
        tk = _largest_tile(rows.c, ATT_TK)
        nq, n_kv, q0, mod_row = rows.c // tq, rows.c // tk, 0, (lambda b: bsz)
    else:
        tq = math.gcd(ATT_TQ, rows.t)
        tk = _largest_tile(rows.c + rows.t, ATT_TK)
        nq, n_kv, q0, mod_row = rows.t // tq, (rows.c + rows.t) // tk, bsz * rows.c // tq, (lambda b: b)
    heads, dv = k.shape[1], k.shape[3]
    x_spec = pl.BlockSpec((tq, d), lambda b, i, j: (q0 + b * nq + i, 0))
    return pl.pallas_call(
        functools.partial(_dif_attn_kernel, dh=dh, lam_init=lam_init),
        grid=(bsz, nq, n_kv),
        in_specs=[
            pl.BlockSpec((2 * heads, tq, dv), lambda b, i, j: (0, q0 + b * nq + i, 0)),
            pl.BlockSpec((1, heads, tk, dv), lambda b, i, j: (b, 0, j, 0)),
            pl.BlockSpec((1, heads, dv, tk), lambda b, i, j: (b, 0, 0, j)),
            x_spec,
            pl.BlockSpec((1, N_MOD, d), lambda b, i, j: (mod_row(b), 0, 0)),
            pl.BlockSpec(lam_p.shape, lambda b, i, j: (0, 0)),
            pl.BlockSpec(subln.shape, lambda b, i, j: (0, 0)),
            _resident(w_out.shape, lambda b, i, j: (0, 0)),
        ],
        out_specs=pl.BlockSpec((tq, d), lambda b, i, j: (b * nq + i, 0)),
        out_shape=jax.ShapeDtypeStruct((bsz * nq * tq, d), F32),
        scratch_shapes=[
            pltpu.VMEM((2 * heads, 1, tq), F32),
            pltpu.VMEM((2 * heads, 1, tq), F32),
            pltpu.VMEM((2 * heads, dv, tq), F32),
            pltpu.VMEM((tq, d), MXU_DTYPE),
            pltpu.VMEM((tk, tq), F32),
            pltpu.VMEM((tk, tq), F32),
            pltpu.VMEM((1, tq), F32),
            pltpu.VMEM((1, tq), F32),
        ],
        compiler_params=_params(("parallel", "parallel", "arbitrary")),
        name="dif_attn_ctx" if ctx_queries else "dif_attn",
    )(q, k, vt, x, mod, lam_p, subln, w_out)


def _diff_tables(rows, tm, dh):
    n_freq = dh // 4
    inv = ROPE_BASE ** (-jnp.arange(n_freq, dtype=F32) / n_freq)
    n_rows = rows.t // GRID_W
    row = jnp.broadcast_to(jnp.arange(n_rows, dtype=F32)[:, None], (n_rows, GRID_W)).reshape(-1)
    col = jnp.broadcast_to(jnp.arange(GRID_W, dtype=F32)[None, :], (n_rows, GRID_W)).reshape(-1)
    ang = jnp.concatenate([row[:, None] * inv, col[:, None] * inv], axis=-1)
    cos, sin = jnp.cos(ang), jnp.sin(ang)
    reps = LANES // dh
    cos_l = jnp.tile(jnp.concatenate([cos, cos], axis=-1), (1, reps))
    sin_l = jnp.tile(jnp.concatenate([-sin, sin], axis=-1), (1, reps))
    ones, zeros = jnp.ones((tm, LANES), F32), jnp.zeros((tm, LANES), F32)
    return jnp.concatenate([ones, cos_l]), jnp.concatenate([zeros, sin_l])


def _mixer_diff(rows, x, mod, g, w_in, lam_p, subln, w_out, lam_init, need_ctx_out):
    d = x.shape[1]
    dh = d // DIFF_HEADS // 2
    cos_t, sin_t = _diff_tables(rows, rows.tile(ATT_ROW_TILE), dh)
    q, k, vt = _dif_in(rows, x, mod, g, _mx(w_in), cos_t, sin_t, dh)
    args = (mod, lam_p, subln.reshape(1, -1), _mx(w_out), lam_init, dh)
    x_lat = _dif_attn(rows, q, k, vt, x, *args, ctx_queries=False)
    x_ctx = _dif_attn(rows, q, k, vt, x, *args, ctx_queries=True) if need_ctx_out else x[:rows.b * rows.c]
    return x_ctx, x_lat


def kernel(x, c, ctx, c_ctx, norm_g, mod_w, mod_b, ffn_w_in, ffn_w_out, lru_w_in, lru_conv_w, lru_conv_b,
           lru_gate_w, lru_gate_b, lru_lam, lru_w_out, ret_w_in, ret_w_out, dif_w_in, dif_lam, dif_subln,
           dif_w_out, final_g):
    bsz, n_lat, d = x.shape
    n_ctx = ctx.shape[1]
    depth = norm_g.shape[0]
    assert bsz + 1 <= SUBLANES
    rows = _Rows(bsz, n_ctx, n_lat)

    cond = jnp.zeros((SUBLANES, d), F32).at[:bsz].set(c).at[bsz].set(c_ctx)
    mods = _ada_mod(cond, mod_w, mod_b).reshape(depth, SUBLANES, N_MOD, d)
    xs = (ctx.reshape(bsz * n_ctx, d), x.reshape(bsz * n_lat, d))

    w_in, w_out = _mx(ffn_w_in), _mx(ffn_w_out)
    for i in range(depth):
        last = i == depth - 1
        mod = mods[i]
        g = norm_g[i].reshape(3, 1, d)
        xs = _ffn(rows, xs, mod, g[0], w_in, w_out, i, 0)
        kind, j = i % 3, i // 3
        if kind == 0:
            xs = _mixer_rglru(rows, xs, mod, g[1], lru_w_in[j], lru_conv_w[j], lru_conv_b[j], lru_gate_w[j],
                              lru_gate_b[j], lru_lam[j], lru_w_out[j])
        elif kind == 1:
            xs = _mixer_retention(rows, xs, mod, g[1], ret_w_in[j], ret_w_out[j])
        else:
            lam_init = 0.8 - 0.6 * math.exp(-0.3 * i)
            xs = _mixer_diff(rows, xs, mod, g[1], dif_w_in[j], dif_lam[j], dif_subln[j], dif_w_out[j],
                             lam_init, not last)
        xs = _ffn(rows, xs, mod, g[2], w_in, w_out, i, 1, final_g=final_g.reshape(1, d) if last else None)
    return xs.reshape(bsz, n_lat, d)
```
